```python
import jax, jax.numpy as jnp
from jax import lax
import numpy as np

D_MODEL = 1024
BATCH = 4
SEQ = 8192
DEPTH = 1

D_MIX = D_MODEL
ATTN_HEADS = 8
HEAD_DIM = 64
D_ATTN = ATTN_HEADS * HEAD_DIM
CONV_GROUPS = 8
D_CONV = D_MIX - D_ATTN
CONV_WIDTH = 3
IDX_HEADS = 8
IDX_DIM = 64
MAX_TOPK = 256
Q_BLOCK = 128
PEER_HEADS = 8
PEER_KEYS = 128
PEER_EXPERTS = PEER_KEYS * PEER_KEYS
PEER_HALF = 128
PEER_QDIM = 2 * PEER_HALF
PEER_TOPK = 16
EPS = 1e-6

SPLITS = [D_ATTN, D_ATTN, D_ATTN, D_CONV, D_CONV, D_CONV, IDX_HEADS * IDX_DIM, IDX_DIM, IDX_HEADS]
D_IN = sum(SPLITS)

kernel_name = "hybrid_dsa_shortconv_peer_block"


def rms_norm(x, w):
    xf = x.astype(jnp.float32)
    y = xf * lax.rsqrt(jnp.mean(xf * xf, axis=-1, keepdims=True) + EPS)
    return (y * w.astype(jnp.float32)).astype(x.dtype)


def _gather_rows(t, idx):
    return jax.vmap(lambda tb, ib: tb[ib])(t, idx)


def dsa_attention(q, k, v, iq, ik, iw):
    B, L = q.shape[0], q.shape[1]
    topk = min(MAX_TOPK, L // 4)
    n_blocks = L // Q_BLOCK
    s_pos = jnp.arange(L)
    att_scale = HEAD_DIM ** -0.5
    idx_scale = (IDX_DIM ** -0.5) * (IDX_HEADS ** -0.5)

    def block_fn(blk):
        start = blk * Q_BLOCK
        qb = lax.dynamic_slice_in_dim(q, start, Q_BLOCK, axis=1)
        iqb = lax.dynamic_slice_in_dim(iq, start, Q_BLOCK, axis=1)
        iwb = lax.dynamic_slice_in_dim(iw, start, Q_BLOCK, axis=1)
        t_pos = start + jnp.arange(Q_BLOCK)
        rel = jax.nn.relu(jnp.einsum('bqhd,bsd->bqhs', iqb, ik).astype(jnp.float32))
        score = jnp.einsum('bqhs,bqh->bqs', rel, iwb.astype(jnp.float32)) * idx_scale
        causal = s_pos[None, None, :] <= t_pos[None, :, None]
        score = jnp.where(causal, score, -jnp.inf)
        _, sel = lax.top_k(score, topk)
        k_sel = _gather_rows(k, sel)
        v_sel = _gather_rows(v, sel)
        logits = jnp.einsum('bqhd,bqkhd->bhqk', qb, k_sel).astype(jnp.float32) * att_scale
        valid = (sel <= t_pos[None, :, None])[:, None, :, :]
        logits = jnp.where(valid, logits, -jnp.inf)
        p = jax.nn.softmax(logits, axis=-1).astype(v.dtype)
        o = jnp.einsum('bhqk,bqkhd->bqhd', p, v_sel)
        return o.reshape(B, Q_BLOCK, ATTN_HEADS * HEAD_DIM)

    out = lax.map(block_fn, jnp.arange(n_blocks))
    return out.transpose(1, 0, 2, 3).reshape(B, L, ATTN_HEADS * HEAD_DIM)


def short_conv(b_gate, c_gate, xc, conv_w, conv_b):
    L = xc.shape[1]
    u = c_gate * xc
    up = jnp.pad(u, ((0, 0), (CONV_WIDTH - 1, 0), (0, 0)))
    y = conv_b
    for j in range(CONV_WIDTH):
        y = y + conv_w[j] * up[:, j:j + L]
    return b_gate * y


def peer_ffn(h, peer_wq, peer_subkeys, peer_u, peer_v):
    B, L, D = h.shape
    n_blocks = L // Q_BLOCK
    hb_all = h.reshape(B, n_blocks, Q_BLOCK, D).transpose(1, 0, 2, 3)

    def block_fn(hb):
        q = (hb @ peer_wq).reshape(B, Q_BLOCK, PEER_HEADS, 2, PEER_HALF)
        s = jnp.einsum('bqhpd,hpnd->bqhpn', q, peer_subkeys).astype(jnp.float32)
        vals, idx = lax.top_k(s, PEER_TOPK)
        combo = vals[..., 0, :, None] + vals[..., 1, None, :]
        ids = idx[..., 0, :, None] * PEER_KEYS + idx[..., 1, None, :]
        combo = combo.reshape(B, Q_BLOCK, PEER_HEADS, PEER_TOPK * PEER_TOPK)
        ids = ids.reshape(B, Q_BLOCK, PEER_HEADS, PEER_TOPK * PEER_TOPK)
        top_s, pos = lax.top_k(combo, PEER_TOPK)
        expert = jnp.take_along_axis(ids, pos, axis=-1)
        g = jax.nn.softmax(top_s, axis=-1).astype(hb.dtype)
        u_sel = peer_u[expert]
        a = jax.nn.gelu(jnp.einsum('bqhkd,bqd->bqhk', u_sel, hb))
        v_sel = peer_v[expert]
        return jnp.einsum('bqhk,bqhkd->bqd', g * a, v_sel)

    out = lax.map(block_fn, hb_all)
    return out.transpose(1, 0, 2, 3).reshape(B, L, D)


def setup_inputs(seed: int = 0) -> dict:
    key = jax.random.key(seed)
    ks = jax.random.split(key, 20)
    f32 = jnp.float32
    nrm = lambda k, shape, s: jax.random.normal(k, shape, f32) * s
    return {
        "x": nrm(ks[0], (BATCH, SEQ, D_MODEL), 1.0),
        "c": nrm(ks[1], (BATCH, D_MODEL), 1.0),
        "norm1_w": 1.0 + nrm(ks[2], (D_MODEL,), 0.02),
        "norm2_w": 1.0 + nrm(ks[3], (D_MODEL,), 0.02),
        "w_ada": nrm(ks[4], (D_MODEL, 6 * D_MODEL), D_MODEL ** -0.5),
        "b_ada": nrm(ks[5], (6 * D_MODEL,), 0.02),
        "w_in": nrm(ks[6], (D_MODEL, D_IN), D_MODEL ** -0.5),
        "q_norm_w": 1.0 + nrm(ks[7], (HEAD_DIM,), 0.02),
        "k_norm_w": 1.0 + nrm(ks[8], (HEAD_DIM,), 0.02),
        "conv_w": nrm(ks[9], (CONV_WIDTH, D_CONV), CONV_WIDTH ** -0.5),
        "conv_b": nrm(ks[10], (D_CONV,), 0.01),
        "attn_out_norm_w": 1.0 + nrm(ks[11], (D_ATTN,), 0.02),
        "conv_out_norm_w": 1.0 + nrm(ks[12], (D_CONV,), 0.02),
        "w_out": nrm(ks[13], (D_MIX, D_MODEL), D_MIX ** -0.5),
        "peer_wq": nrm(ks[14], (D_MODEL, PEER_HEADS * PEER_QDIM), D_MODEL ** -0.5),
        "peer_subkeys": nrm(ks[15], (PEER_HEADS, 2, PEER_KEYS, PEER_HALF), PEER_HALF ** -0.5),
        "peer_u": nrm(ks[16], (PEER_EXPERTS, D_MODEL), D_MODEL ** -0.5),
        "peer_v": nrm(ks[17], (PEER_EXPERTS, D_MODEL), PEER_HEADS ** -0.5),
    }


def reference(x, c, norm1_w, norm2_w, w_ada, b_ada, w_in, q_norm_w, k_norm_w,
              conv_w, conv_b, attn_out_norm_w, conv_out_norm_w, w_out,
              peer_wq, peer_subkeys, peer_u, peer_v):
    B, L, _ = x.shape
    mod = jax.nn.silu(c) @ w_ada + b_ada
    shift1, scale1, gate1, shift2, scale2, gate2 = [m[:, None, :] for m in jnp.split(mod, 6, axis=-1)]

    for _ in range(DEPTH):
        h = rms_norm(x, norm1_w) * (1.0 + scale1) + shift1
        proj = h @ w_in
        offs = np.cumsum(SPLITS)[:-1].tolist()
        q, k, v, cb, cc, cx, iq, ik, iw = jnp.split(proj, offs, axis=-1)
        q = rms_norm(q.reshape(B, L, ATTN_HEADS, HEAD_DIM), q_norm_w)
        k = rms_norm(k.reshape(B, L, ATTN_HEADS, HEAD_DIM), k_norm_w)
        v = v.reshape(B, L, ATTN_HEADS, HEAD_DIM)
        iq = iq.reshape(B, L, IDX_HEADS, IDX_DIM)
        y_attn = dsa_attention(q, k, v, iq, ik, iw)
        y_conv = short_conv(cb, cc, cx, conv_w, conv_b)
        y_mix = jnp.concatenate([rms_norm(y_attn, attn_out_norm_w),
                                 rms_norm(y_conv, conv_out_norm_w)], axis=-1)
        x = x + gate1 * (y_mix @ w_out)
        h2 = rms_norm(x, norm2_w) * (1.0 + scale2) + shift2
        x = x + gate2 * peer_ffn(h2, peer_wq, peer_subkeys, peer_u, peer_v)
    return x
```

```python
import functools

import jax
import jax.numpy as jnp
from jax import lax
from jax.experimental import pallas as pl
from jax.experimental.pallas import tpu as pltpu

F32 = jnp.float32
_MXU = jnp.bfloat16

EPS = 1e-6
ATTN_HEADS = 8
HEAD_DIM = 64
IDX_HEADS = 8
IDX_DIM = 64
MAX_TOPK = 256
PEER_HEADS = 8
PEER_KEYS = 128
PEER_TOPK = 16
NEG = -1e30

TILE = 256
VMEM_LIMIT = 56 * 1024 * 1024


def _dot(a, b):
    return jnp.dot(a, b, preferred_element_type=F32)


def _cparams(sem):
    return pltpu.CompilerParams(dimension_semantics=sem, vmem_limit_bytes=VMEM_LIMIT)


def _ada_kernel(c_ref, w_ref, b_ref, o_ref):
    c = c_ref[...]
    s = c * jax.nn.sigmoid(c)
    o_ref[...] = _dot(s.astype(_MXU), w_ref[...].astype(_MXU)) + b_ref[...]


def _ada(c_pad, w_ada, b_ada):
    rows, d = c_pad.shape
    n = w_ada.shape[1]
    tn = 1024
    return pl.pallas_call(
        _ada_kernel,
        grid=(n // tn,),
        in_specs=[pl.BlockSpec((rows, d), lambda j: (0, 0)),
                  pl.BlockSpec((d, tn), lambda j: (0, j)),
                  pl.BlockSpec((1, tn), lambda j: (0, j))],
        out_specs=pl.BlockSpec((rows, tn), lambda j: (0, j)),
        out_shape=jax.ShapeDtypeStruct((rows, n), F32),
        compiler_params=_cparams(("arbitrary",)),
        name="ada",
    )(c_pad, w_ada, b_ada)


def _split_pairs(zt, out_ref):
    t = zt.shape[1]
    row = lax.broadcasted_iota(jnp.int32, (128, t), 0)
    for p in range(4):
        pair = zt[p * 128:(p + 1) * 128]
        out_ref[2 * p] = jnp.where(row < 64, pair, 0.0).astype(out_ref.dtype)
        out_ref[2 * p + 1] = jnp.where(row >= 64, pair, 0.0).astype(out_ref.dtype)


def _mix_in_kernel(x_ref, mod_ref, n1_ref, w_ref, qnw_ref, knw_ref, cw_ref, cb_ref, cnw_ref, ones_ref,
                   qT_ref, k_ref, vT_ref, iqT_ref, ik_ref, iwT_ref, yc_ref, prev_ref,
                   *, att_scale, idx_scale):
    li = pl.program_id(1)
    x = x_ref[...]
    t = x.shape[0]
    ms = jnp.mean(x * x, axis=-1, keepdims=True)
    y = x * lax.rsqrt(ms + EPS) * n1_ref[...]
    h = (y * (1.0 + mod_ref[1:2, :]) + mod_ref[0:1, :]).astype(_MXU)

    def proj(c0, c1):
        return _dot(h, w_ref[:, c0:c1])

    ones = ones_ref[...]

    def head_norm(z, w):
        z2 = z * z
        hi = z2.astype(_MXU)
        lo = (z2 - hi.astype(F32)).astype(_MXU)
        ss = _dot(hi, ones) + _dot(lo, ones)
        return z * lax.rsqrt(ss * (1.0 / HEAD_DIM) + EPS) * w

    q = head_norm(proj(0, 512), qnw_ref[...]) * att_scale
    _split_pairs(q.T, qT_ref)
    k = head_norm(proj(512, 1024), knw_ref[...])
    k_ref[...] = k.astype(k_ref.dtype)
    vT_ref[...] = proj(1024, 1536).T.astype(vT_ref.dtype)
    _split_pairs(proj(3072, 3584).T, iqT_ref)
    tail = proj(3584, 3840)
    ik_ref[...] = tail[:, 0:128].astype(ik_ref.dtype)
    iwT_ref[...] = tail[:, 128:256].T[0:IDX_HEADS] * idx_scale

    cgate = proj(1536, 2048)
    u = proj(2048, 2560) * proj(2560, 3072)

    @pl.when(li == 0)
    def _():
        prev_ref[...] = jnp.zeros_like(prev_ref)

    prev = prev_ref[...]
    row = lax.broadcasted_iota(jnp.int32, (t, 1), 0)
    u1 = jnp.where(row == 0, prev[7:8], pltpu.roll(u, 1, 0))
    u2 = jnp.where(row == 0, prev[6:7], jnp.where(row == 1, prev[7:8], pltpu.roll(u, 2, 0)))
    prev_ref[...] = u[t - 8:t]
    yc = cgate * (cb_ref[...] + cw_ref[0:1] * u2 + cw_ref[1:2] * u1 + cw_ref[2:3] * u)
    msc = jnp.mean(yc * yc, axis=-1, keepdims=True)
    yc_ref[...] = (yc * lax.rsqrt(msc + EPS) * cnw_ref[...]).astype(yc_ref.dtype)


def _mix_in(x2, mod3, norm1_w, w_cat, qnw, knw, conv_w, conv_b, cnw, ones_bd, *, B, L, att_scale, idx_scale):
    n, d = x2.shape
    t = TILE
    nl = L // t
    full = lambda shape: pl.BlockSpec(shape, lambda b, l: (0,) * len(shape))
    tok = lambda w: pl.BlockSpec((t, w), lambda b, l: (b * nl + l, 0))
    out_shapes = (
        jax.ShapeDtypeStruct((B, ATTN_HEADS, 128, L), _MXU),
        jax.ShapeDtypeStruct((n, 512), _MXU),
        jax.ShapeDtypeStruct((B, nl, 512, t), _MXU),
        jax.ShapeDtypeStruct((B, IDX_HEADS, 128, L), _MXU),
        jax.ShapeDtypeStruct((n, 128), _MXU),
        jax.ShapeDtypeStruct((B, IDX_HEADS, L), F32),
        jax.ShapeDtypeStruct((n, 512), _MXU),
    )
    out_specs = (
        pl.BlockSpec((None, ATTN_HEADS, 128, t), lambda b, l: (b, 0, 0, l)),
        tok(512),
        pl.BlockSpec((None, None, 512, t), lambda b, l: (b, l, 0, 0)),
        pl.BlockSpec((None, IDX_HEADS, 128, t), lambda b, l: (b, 0, 0, l)),
        tok(128),
        pl.BlockSpec((None, IDX_HEADS, t), lambda b, l: (b, 0, l)),
        tok(512),
    )
    return pl.pallas_call(
        functools.partial(_mix_in_kernel, att_scale=att_scale, idx_scale=idx_scale),
        grid=(B, nl),
        in_specs=[tok(d),
                  pl.BlockSpec((None, 6, d), lambda b, l: (b, 0, 0)),
                  full((1, d)), full(w_cat.shape), full((1, 512)), full((1, 512)),
                  full((3, 512)), full((1, 512)), full((1, 512)), full((512, 512))],
        out_specs=out_specs,
        out_shape=out_shapes,
        scratch_shapes=[pltpu.VMEM((8, 512), F32)],
        compiler_params=_cparams(("arbitrary", "arbitrary")),
        name="mix_in",
    )(x2, mod3, norm1_w, w_cat, qnw, knw, conv_w, conv_b, cnw, ones_bd)


def _dsa_kernel(qT_ref, iqT_ref, iwT_ref, k_ref, vT_ref, ik_ref, anw_ref, o_ref,
                I_ref, m_ref, l_ref, acc_ref, *, topk, seq_len):
    t = TILE
    qb = pl.program_id(1)
    nk = qb + 1
    kf = float(topk)
    inf = float("inf")
    qpos = qb * t + lax.broadcasted_iota(jnp.int32, (1, t), 1)
    rowi = lax.broadcasted_iota(jnp.int32, (t, 1), 0)

    def p1(c, carry):
        mn, mx = carry
        ikc = ik_ref[c]
        acc = jnp.zeros((t, t), F32)
        for h in range(IDX_HEADS):
            r = _dot(ikc, iqT_ref[h])
            acc = acc + iwT_ref[h:h + 1, :] * jnp.maximum(r, 0.0)
        causal = (c * t + rowi) <= qpos
        I_ref[c] = jnp.where(causal, acc, -inf)
        mn = jnp.minimum(mn, jnp.min(jnp.where(causal, acc, inf), axis=0, keepdims=True))
        mx = jnp.maximum(mx, jnp.max(jnp.where(causal, acc, -inf), axis=0, keepdims=True))
        return mn, mx

    mn, mx = lax.fori_loop(0, nk, p1, (jnp.full((1, t), inf, F32), jnp.full((1, t), -inf, F32)))

    def count(pred):
        def body(c, s):
            return s + jnp.sum(jnp.where(pred(I_ref[c], c), 1.0, 0.0), axis=0, keepdims=True)
        return lax.fori_loop(0, nk, body, jnp.zeros((1, t), F32))

    def min_where(pred):
        def body(c, s):
            blk = I_ref[c]
            return jnp.minimum(s, jnp.min(jnp.where(pred(blk), blk, inf), axis=0, keepdims=True))
        return lax.fori_loop(0, nk, body, jnp.full((1, t), inf, F32))

    done0 = jnp.where(qpos + 1 > topk, 0.0, 1.0)
    tau0 = jnp.full((1, t), -inf, F32)
    hi0 = mx + (jnp.abs(mx) + 1.0)

    def cond_a(st):
        it, _, _, _, done = st
        return jnp.logical_and(it < 40, jnp.min(done) < 0.5)

    def body_a(st):
        it, lo, hi, tau, done = st
        mid = lo + (hi - lo) * 0.5
        c = count(lambda blk, _: blk >= mid)
        hit = jnp.logical_and(c == kf, done < 0.5)
        tau = jnp.where(hit, mid, tau)
        done = jnp.where(hit, 1.0, done)
        ge = c >= kf
        return it + 1, jnp.where(ge, mid, lo), jnp.where(ge, hi, mid), tau, done

    _, lo, hi, tau, done = lax.while_loop(cond_a, body_a, (jnp.int32(0), mn, hi0, tau0, done0))

    def cond_b(st):
        it, _, _, _, done, _, _ = st
        return jnp.logical_and(it < 4096, jnp.min(done) < 0.5)

    def body_b(st):
        it, lo, hi, tau, done, tie, need = st
        vlo = min_where(lambda blk: blk >= lo)
        ngt = count(lambda blk, _: blk > vlo)
        is_tie = jnp.logical_and(ngt < kf, done < 0.5)
        tau = jnp.where(is_tie, vlo, tau)
        need = jnp.where(is_tie, kf - ngt, need)
        tie = jnp.where(is_tie, 1.0, tie)
        done = jnp.where(is_tie, 1.0, done)
        lo2 = min_where(lambda blk: blk > vlo)
        mid = lo2 + (hi - lo2) * 0.5
        c = count(lambda blk, _: blk >= mid)
        hit = jnp.logical_and(c == kf, done < 0.5)
        tau = jnp.where(hit, mid, tau)
        done = jnp.where(hit, 1.0, done)
        ge = c >= kf
        return it + 1, jnp.where(ge, mid, lo2), jnp.where(ge, hi, mid), tau, done, tie, need

    zeros = jnp.zeros((1, t), F32)
    _, _, _, tau, done, tie, need = lax.while_loop(
        cond_b, body_b, (jnp.int32(0), lo, hi, tau, done, zeros, zeros))

    nbits = int(seq_len).bit_length() + 1
    ntrip = jnp.where(jnp.max(tie) > 0.5, nbits, 0)

    def body_j(_, st):
        jlo, jhi = st
        jm = (jlo + jhi) >> 1
        c = count(lambda blk, cc: jnp.logical_and(blk == tau, (cc * t + rowi) <= jm))
        ok = c >= need
        return jnp.where(ok, jlo, jm), jnp.where(ok, jm, jhi)

    _, jhi = lax.fori_loop(0, ntrip, body_j,
                           (jnp.full((1, t), -1, jnp.int32), jnp.full((1, t), seq_len - 1, jnp.int32)))
    jstar = jnp.where(tie > 0.5, jhi, -1)

    m_ref[...] = jnp.full(m_ref.shape, NEG, F32)
    l_ref[...] = jnp.zeros(l_ref.shape, F32)
    acc_ref[...] = jnp.zeros(acc_ref.shape, F32)

    def p3(c, carry):
        blk = I_ref[c]
        sel = jnp.logical_or(blk > tau, jnp.logical_and(blk == tau, (c * t + rowi) <= jstar))
        bias = jnp.where(sel, 0.0, NEG)
        for h in range(ATTN_HEADS):
            p = h // 2
            s = _dot(k_ref[c, :, p * 128:(p + 1) * 128], qT_ref[h]) + bias
            m_old = m_ref[h:h + 1, :]
            m_new = jnp.maximum(m_old, jnp.max(s, axis=0, keepdims=True))
            alpha = jnp.exp(m_old - m_new)
            e = jnp.exp(s - m_new)
            l_ref[h:h + 1, :] = alpha * l_ref[h:h + 1, :] + jnp.sum(e, axis=0, keepdims=True)
            pv = _dot(vT_ref[c, h * HEAD_DIM:(h + 1) * HEAD_DIM, :], e.astype(_MXU))
            acc_ref[h * HEAD_DIM:(h + 1) * HEAD_DIM, :] = alpha * acc_ref[h * HEAD_DIM:(h + 1) * HEAD_DIM, :] + pv
            m_ref[h:h + 1, :] = m_new
        return carry

    lax.fori_loop(0, nk, p3, 0)

    for h in range(ATTN_HEADS):
        acc_ref[h * HEAD_DIM:(h + 1) * HEAD_DIM, :] = acc_ref[h * HEAD_DIM:(h + 1) * HEAD_DIM, :] / l_ref[h:h + 1, :]
    yt = acc_ref[...]
    ms = jnp.mean(yt * yt, axis=0, keepdims=True)
    o_ref[...] = (yt * lax.rsqrt(ms + EPS) * anw_ref[...]).T.astype(o_ref.dtype)


def _dsa(qT, iqT, iwT, k4, vT4, ik4, anw_col, *, B, L, topk):
    t = TILE
    nl = L // t
    n = B * L
    return pl.pallas_call(
        functools.partial(_dsa_kernel, topk=topk, seq_len=L),
        grid=(B, nl),
        in_specs=[pl.BlockSpec((None, ATTN_HEADS, 128, t), lambda b, q: (b, 0, 0, q)),
                  pl.BlockSpec((None, IDX_HEADS, 128, t), lambda b, q: (b, 0, 0, q)),
                  pl.BlockSpec((None, IDX_HEADS, t), lambda b, q: (b, 0, q)),
                  pl.BlockSpec((None, nl, t, 512), lambda b, q: (b, 0, 0, 0), pipeline_mode=pl.Buffered(1)),
                  pl.BlockSpec((None, nl, 512, t), lambda b, q: (b, 0, 0, 0), pipeline_mode=pl.Buffered(1)),
                  pl.BlockSpec((None, nl, t, 128), lambda b, q: (b, 0, 0, 0), pipeline_mode=pl.Buffered(1)),
                  pl.BlockSpec((512, 1), lambda b, q: (0, 0))],
        out_specs=pl.BlockSpec((t, 512), lambda b, q: (b * nl + q, 0)),
        out_shape=jax.ShapeDtypeStruct((n, 512), _MXU),
        scratch_shapes=[pltpu.VMEM((nl, t, t), F32),
                        pltpu.VMEM((ATTN_HEADS, t), F32),
                        pltpu.VMEM((ATTN_HEADS, t), F32),
                        pltpu.VMEM((512, t), F32)],
        compiler_params=_cparams(("arbitrary", "arbitrary")),
        name="dsa",
    )(qT, iqT, iwT, k4, vT4, ik4, anw_col)


def _mix_out_kernel(x_ref, ya_ref, yc_ref, mod_ref, woa_ref, woc_ref, n2_ref, wqT_ref, sk_ref,
                    x1_ref, h2T_ref, sT_ref):
    proj = _dot(ya_ref[...], woa_ref[...]) + _dot(yc_ref[...], woc_ref[...])
    x1 = x_ref[...] + mod_ref[2:3, :] * proj
    x1_ref[...] = x1
    ms = jnp.mean(x1 * x1, axis=-1, keepdims=True)
    h2 = x1 * lax.rsqrt(ms + EPS) * n2_ref[...] * (1.0 + mod_ref[4:5, :]) + mod_ref[3:4, :]
    h2t = h2.T.astype(_MXU)
    h2T_ref[...] = h2t
    qpt = _dot(wqT_ref[...], h2t).astype(_MXU)
    for hp in range(2 * PEER_HEADS):
        sT_ref[hp] = _dot(sk_ref[hp], qpt[hp * 128:(hp + 1) * 128])


def _mix_out(x2, ya, yc, mod3, woa, woc, norm2_w, wqT, sk, *, B, L):
    n, d = x2.shape
    t = TILE
    nl = L // t
    full = lambda shape: pl.BlockSpec(shape, lambda i: (0,) * len(shape))
    return pl.pallas_call(
        _mix_out_kernel,
        grid=(n // t,),
        in_specs=[pl.BlockSpec((t, d), lambda i: (i, 0)),
                  pl.BlockSpec((t, 512), lambda i: (i, 0)),
                  pl.BlockSpec((t, 512), lambda i: (i, 0)),
                  pl.BlockSpec((None, 6, d), lambda i: (i // nl, 0, 0)),
                  full(woa.shape), full(woc.shape), full((1, d)), full(wqT.shape), full(sk.shape)],
        out_specs=(pl.BlockSpec((t, d), lambda i: (i, 0)),
                   pl.BlockSpec((d, t), lambda i: (0, i)),
                   pl.BlockSpec((2 * PEER_HEADS, PEER_KEYS, t), lambda i: (0, 0, i))),
        out_shape=(jax.ShapeDtypeStruct((n, d), F32),
                   jax.ShapeDtypeStruct((d, n), _MXU),
                   jax.ShapeDtypeStruct((2 * PEER_HEADS, PEER_KEYS, n), F32)),
        compiler_params=_cparams(("arbitrary",)),
        name="mix_out",
    )(x2, ya, yc, mod3, woa, woc, norm2_w, wqT, sk)


def _extract_topk(vals, n_take):
    r, t = vals.shape
    rowi = lax.broadcasted_iota(jnp.int32, (r, t), 0)
    work = vals
    rank = jnp.full((r, t), 99.0, F32)
    taken = []
    for kk in range(n_take):
        m = jnp.max(work, axis=0, keepdims=True)
        idx = jnp.min(jnp.where(work == m, rowi, r), axis=0, keepdims=True)
        sel = rowi == idx
        rank = jnp.where(sel, float(kk), rank)
        work = jnp.where(sel, -float("inf"), work)
        taken.append(m)
    return taken, rank


def _route_kernel(sT_ref, r2_ref, f2_ref, c1_ref, e1_ref):
    kk = PEER_TOPK
    for h in range(PEER_HEADS):
        s1 = sT_ref[2 * h]
        s2 = sT_ref[2 * h + 1]
        v1, rank1 = _extract_topk(s1, kk)
        v2, rank2 = _extract_topk(s2, kk)
        v2all = jnp.concatenate(v2, axis=0)
        combo = jnp.concatenate([v1[a] + v2all for a in range(kk)], axis=0)
        _, crank = _extract_topk(combo, kk)
        selc = jnp.where(crank < float(kk), 1.0, 0.0)
        z = jnp.sum(selc * jnp.exp(combo - combo[0:1]), axis=0, keepdims=True)
        cnt1 = jnp.zeros_like(s1)
        for a in range(kk):
            cnt_a = jnp.sum(selc[a * kk:(a + 1) * kk], axis=0, keepdims=True)
            cnt1 = cnt1 + jnp.where(rank1 == float(a), cnt_a, 0.0)
        r2_ref[h] = rank2.astype(r2_ref.dtype)
        f2_ref[h] = jnp.where(rank2 < float(kk), jnp.exp(s2 - v2[0]), 0.0).astype(f2_ref.dtype)
        c1_ref[h] = cnt1
        e1_ref[h] = jnp.where(rank1 < float(kk), jnp.exp(s1 - v1[0]), 0.0) / z


def _route(sT):
    hp, nkeys, n = sT.shape
    t = 128
    spec = pl.BlockSpec((PEER_HEADS, nkeys, t), lambda i: (0, 0, i))
    shp = lambda dt: jax.ShapeDtypeStruct((PEER_HEADS, nkeys, n), dt)
    return pl.pallas_call(
        _route_kernel,
        grid=(n // t,),
        in_specs=[pl.BlockSpec((hp, nkeys, t), lambda i: (0, 0, i))],
        out_specs=(spec, spec, spec, spec),
        out_shape=(shp(_MXU), shp(_MXU), shp(F32), shp(F32)),
        compiler_params=_cparams(("arbitrary",)),
        name="route",
    )(sT)


PEER_TM = 512
PEER_TE = 512


def _gelu(a):
    return 0.5 * a * (1.0 + jnp.tanh(0.7978845608028654 * (a + 0.044715 * (a * a * a))))


def _peer_kernel(h2T_ref, r2_ref, f2_ref, c1_ref, e1_ref, u_ref, vT_ref, x1_ref, mod_ref, o_ref,
                 acc_ref, g_ref):
    e = pl.program_id(1)
    sub = PEER_TE // PEER_KEYS

    @pl.when(e == 0)
    def _():
        acc_ref[...] = jnp.zeros_like(acc_ref)

    a_all = _dot(u_ref[...], h2T_ref[...])
    for ii in range(sub):
        i = e * sub + ii
        w = jnp.zeros((PEER_KEYS, PEER_TM), _MXU)
        for h in range(PEER_HEADS):
            c1 = c1_ref[h, pl.ds(i, 1), :].astype(_MXU)
            e1 = e1_ref[h, pl.ds(i, 1), :].astype(_MXU)
            w = w + jnp.where(r2_ref[h] < c1, f2_ref[h], jnp.zeros((), _MXU)) * e1
        g = _gelu(a_all[ii * PEER_KEYS:(ii + 1) * PEER_KEYS]) * w.astype(F32)
        g_ref[ii * PEER_KEYS:(ii + 1) * PEER_KEYS, :] = g.astype(_MXU)
    acc_ref[...] += _dot(vT_ref[...], g_ref[...])

    @pl.when(e == pl.num_programs(1) - 1)
    def _():
        o_ref[...] = x1_ref[...] + mod_ref[5:6, :] * acc_ref[...].T


def _peer(h2T, r2, f2, c1, e1, u_b, vT_b, x1, mod3, *, B, L):
    d, n = h2T.shape
    ne = u_b.shape[0]
    tm, te = PEER_TM, PEER_TE
    ntl = L // tm
    rspec = pl.BlockSpec((PEER_HEADS, PEER_KEYS, tm), lambda i, e: (0, 0, i))
    return pl.pallas_call(
        _peer_kernel,
        grid=(n // tm, ne // te),
        in_specs=[pl.BlockSpec((d, tm), lambda i, e: (0, i)),
                  rspec, rspec, rspec, rspec,
                  pl.BlockSpec((te, d), lambda i, e: (e, 0)),
                  pl.BlockSpec((d, te), lambda i, e: (0, e)),
                  pl.BlockSpec((tm, d), lambda i, e: (i, 0)),
                  pl.BlockSpec((None, 6, d), lambda i, e: (i // ntl, 0, 0))],
        out_specs=pl.BlockSpec((tm, d), lambda i, e: (i, 0)),
        out_shape=jax.ShapeDtypeStruct((n, d), F32),
        scratch_shapes=[pltpu.VMEM((d, tm), F32), pltpu.VMEM((te, tm), _MXU)],
        compiler_params=_cparams(("arbitrary", "arbitrary")),
        name="peer",
    )(h2T, r2, f2, c1, e1, u_b, vT_b, x1, mod3)


def kernel(x, c, norm1_w, norm2_w, w_ada, b_ada, w_in, q_norm_w, k_norm_w, conv_w, conv_b,
           attn_out_norm_w, conv_out_norm_w, w_out, peer_wq, peer_subkeys, peer_u, peer_v):
    B, L, D = x.shape
    n = B * L
    assert D == 1024 and L % PEER_TM == 0 and L % TILE == 0
    topk = min(MAX_TOPK, L // 4)
    att_scale = HEAD_DIM ** -0.5
    idx_scale = (IDX_DIM ** -0.5) * (IDX_HEADS ** -0.5)

    c_pad = jnp.pad(c, ((0, 8 - B % 8 if B % 8 else 0), (0, 0)))
    w_ik = w_in[:, 3584:3648]
    w_iw = jnp.pad(w_in[:, 3648:3656], ((0, 0), (0, 120)))
    w_cat = jnp.concatenate([w_in[:, :3584], w_ik, w_ik, w_iw], axis=1).astype(_MXU)
    tile8 = lambda w: jnp.tile(w, ATTN_HEADS).reshape(1, 512)
    head_id = jnp.arange(512) // HEAD_DIM
    ones_bd = (head_id[:, None] == head_id[None, :]).astype(_MXU)
    woa = w_out[:512].astype(_MXU)
    woc = w_out[512:].astype(_MXU)
    wqT = peer_wq.T.astype(_MXU)
    sk = peer_subkeys.reshape(2 * PEER_HEADS, PEER_KEYS, PEER_KEYS).astype(_MXU)
    u_b = peer_u.astype(_MXU)
    vT_b = peer_v.T.astype(_MXU)

    mod = _ada(c_pad, w_ada, b_ada.reshape(1, -1))[:B]
    mod3 = mod.reshape(B, 6, D)
    x2 = x.reshape(n, D)

    qT, k, vT4, iqT, ik, iwT, yc = _mix_in(
        x2, mod3, norm1_w.reshape(1, D), w_cat, tile8(q_norm_w), tile8(k_norm_w),
        conv_w, conv_b.reshape(1, 512), conv_out_norm_w.reshape(1, 512), ones_bd,
        B=B, L=L, att_scale=att_scale, idx_scale=idx_scale)
    nl = L // TILE
    ya = _dsa(qT, iqT, iwT, k.reshape(B, nl, TILE, 512), vT4, ik.reshape(B, nl, TILE, 128),
              attn_out_norm_w.reshape(512, 1), B=B, L=L, topk=topk)
    x1, h2T, sT = _mix_out(x2, ya, yc, mod3, woa, woc, norm2_w.reshape(1, D), wqT, sk, B=B, L=L)
    r2, f2, c1, e1 = _route(sT)
    out = _peer(h2T, r2, f2, c1, e1, u_b, vT_b, x1, mod3, B=B, L=L)
    return out.reshape(B, L, D)
```

```python
import functools

import jax
import jax.numpy as jnp
from jax import lax
from jax.experimental import pallas as pl
from jax.experimental.pallas import tpu as pltpu

F32 = jnp.float32
_MXU = jnp.bfloat16

EPS = 1e-6
ATTN_HEADS = 8
HEAD_DIM = 64
IDX_HEADS = 8
IDX_DIM = 64
MAX_TOPK = 256
PEER_HEADS = 8
PEER_KEYS = 128
PEER_TOPK = 16
NEG = -1e30
LOG2E = 1.4426950408889634

TILE = 256
VMEM_LIMIT = 56 * 1024 * 1024


def _dot(a, b):
    return jnp.dot(a, b, preferred_element_type=F32)


def _cparams(sem, flags=None):
    return pltpu.CompilerParams(dimension_semantics=sem, vmem_limit_bytes=VMEM_LIMIT, flags=flags)


def _ada_kernel(c_ref, w_ref, b_ref, o_ref):
    c = c_ref[...]
    s = c * jax.nn.sigmoid(c)
    o_ref[...] = _dot(s.astype(_MXU), w_ref[...].astype(_MXU)) + b_ref[...]


def _ada(c_pad, w_ada, b_ada):
    rows, d = c_pad.shape
    n = w_ada.shape[1]
    tn = 1024
    return pl.pallas_call(
        _ada_kernel,
        grid=(n // tn,),
        in_specs=[pl.BlockSpec((rows, d), lambda j: (0, 0)),
                  pl.BlockSpec((d, tn), lambda j: (0, j)),
                  pl.BlockSpec((1, tn), lambda j: (0, j))],
        out_specs=pl.BlockSpec((rows, tn), lambda j: (0, j)),
        out_shape=jax.ShapeDtypeStruct((rows, n), F32),
        compiler_params=_cparams(("arbitrary",)),
        name="ada",
    )(c_pad, w_ada, b_ada)


def _split_pairs(zt, out_ref):
    t = zt.shape[1]
    row = lax.broadcasted_iota(jnp.int32, (128, t), 0)
    for p in range(4):
        pair = zt[p * 128:(p + 1) * 128]
        out_ref[2 * p] = jnp.where(row < 64, pair, 0.0).astype(out_ref.dtype)
        out_ref[2 * p + 1] = jnp.where(row >= 64, pair, 0.0).astype(out_ref.dtype)


def _mix_in_kernel(x_ref, mod_ref, n1_ref, w_ref, qnw_ref, knw_ref, cw_ref, cb_ref, cnw_ref, ones_ref,
                   qT_ref, k_ref, vT_ref, iqT_ref, ik_ref, iwT_ref, yc_ref, prev_ref,
                   *, att_scale, idx_scale):
    li = pl.program_id(1)
    x = x_ref[...]
    t = x.shape[0]
    ms = jnp.mean(x * x, axis=-1, keepdims=True)
    y = x * lax.rsqrt(ms + EPS) * n1_ref[...]
    h = (y * (1.0 + mod_ref[1:2, :]) + mod_ref[0:1, :]).astype(_MXU)

    def proj(c0, c1):
        return _dot(h, w_ref[:, c0:c1])

    ones = ones_ref[...]

    def head_norm(z, w):
        z2 = z * z
        hi = z2.astype(_MXU)
        lo = (z2 - hi.astype(F32)).astype(_MXU)
        ss = _dot(hi, ones) + _dot(lo, ones)
        return z * lax.rsqrt(ss * (1.0 / HEAD_DIM) + EPS) * w

    q = head_norm(proj(0, 512), qnw_ref[...]) * att_scale
    _split_pairs(q.T, qT_ref)
    k = head_norm(proj(512, 1024), knw_ref[...])
    k_ref[...] = k.astype(k_ref.dtype)
    vT_ref[...] = proj(1024, 1536).T.astype(vT_ref.dtype)
    _split_pairs(proj(3072, 3584).T, iqT_ref)
    tail = proj(3584, 3840)
    ik_ref[...] = tail[:, 0:128].astype(ik_ref.dtype)
    iwT_ref[...] = tail[:, 128:256].T[0:IDX_HEADS] * idx_scale

    cgate = proj(1536, 2048)
    u = proj(2048, 2560) * proj(2560, 3072)

    @pl.when(li == 0)
    def _():
        prev_ref[...] = jnp.zeros_like(prev_ref)

    prev = prev_ref[...]
    row = lax.broadcasted_iota(jnp.int32, (t, 1), 0)
    u1 = jnp.where(row == 0, prev[7:8], pltpu.roll(u, 1, 0))
    u2 = jnp.where(row == 0, prev[6:7], jnp.where(row == 1, prev[7:8], pltpu.roll(u, 2, 0)))
    prev_ref[...] = u[t - 8:t]
    yc = cgate * (cb_ref[...] + cw_ref[0:1] * u2 + cw_ref[1:2] * u1 + cw_ref[2:3] * u)
    msc = jnp.mean(yc * yc, axis=-1, keepdims=True)
    yc_ref[...] = (yc * lax.rsqrt(msc + EPS) * cnw_ref[...]).astype(yc_ref.dtype)


def _mix_in(x2, mod3, norm1_w, w_cat, qnw, knw, conv_w, conv_b, cnw, ones_bd, *, B, L, att_scale, idx_scale):
    n, d = x2.shape
    t = TILE
    nl = L // t
    full = lambda shape: pl.BlockSpec(shape, lambda b, l: (0,) * len(shape))
    tok = lambda w: pl.BlockSpec((t, w), lambda b, l: (b * nl + l, 0))
    out_shapes = (
        jax.ShapeDtypeStruct((B, ATTN_HEADS, 128, L), _MXU),
        jax.ShapeDtypeStruct((n, 512), _MXU),
        jax.ShapeDtypeStruct((B, nl, 512, t), _MXU),
        jax.ShapeDtypeStruct((B, IDX_HEADS, 128, L), _MXU),
        jax.ShapeDtypeStruct((n, 128), _MXU),
        jax.ShapeDtypeStruct((B, IDX_HEADS, L), F32),
        jax.ShapeDtypeStruct((n, 512), _MXU),
    )
    out_specs = (
        pl.BlockSpec((None, ATTN_HEADS, 128, t), lambda b, l: (b, 0, 0, l)),
        tok(512),
        pl.BlockSpec((None, None, 512, t), lambda b, l: (b, l, 0, 0)),
        pl.BlockSpec((None, IDX_HEADS, 128, t), lambda b, l: (b, 0, 0, l)),
        tok(128),
        pl.BlockSpec((None, IDX_HEADS, t), lambda b, l: (b, 0, l)),
        tok(512),
    )
    return pl.pallas_call(
        functools.partial(_mix_in_kernel, att_scale=att_scale, idx_scale=idx_scale),
        grid=(B, nl),
        in_specs=[tok(d),
                  pl.BlockSpec((None, 6, d), lambda b, l: (b, 0, 0)),
                  full((1, d)), full(w_cat.shape), full((1, 512)), full((1, 512)),
                  full((3, 512)), full((1, 512)), full((1, 512)), full((512, 512))],
        out_specs=out_specs,
        out_shape=out_shapes,
        scratch_shapes=[pltpu.VMEM((8, 512), F32)],
        compiler_params=_cparams(("arbitrary", "arbitrary")),
        name="mix_in",
    )(x2, mod3, norm1_w, w_cat, qnw, knw, conv_w, conv_b, cnw, ones_bd)


def _dsa_kernel(qT_ref, iqT_ref, iwT_ref, k_ref, vT_ref, ik_ref, anw_ref, qnw_ref, knw_ref, o_ref,
                I_ref, m_ref, l_ref, acc_ref, bias_ref, s_ref, *, topk, seq_len, att_scale):
    t = TILE
    qb = pl.program_id(1)
    nk = qb + 1
    kf = float(topk)
    inf = float("inf")
    qpos = qb * t + lax.broadcasted_iota(jnp.int32, (1, t), 1)
    rowi = lax.broadcasted_iota(jnp.int32, (t, 1), 0)

    def p1(c, carry):
        mn, mx = carry
        ikc = ik_ref[c]
        acc = jnp.zeros((t, t), F32)
        for h in range(IDX_HEADS):
            r = _dot(ikc, iqT_ref[h])
            acc = acc + iwT_ref[h:h + 1, :] * jnp.maximum(r, 0.0)
        causal = (c * t + rowi) <= qpos
        I_ref[c] = jnp.where(causal, acc, -inf)
        mn = jnp.minimum(mn, jnp.min(jnp.where(causal, acc, inf), axis=0, keepdims=True))
        mx = jnp.maximum(mx, jnp.max(jnp.where(causal, acc, -inf), axis=0, keepdims=True))
        return mn, mx

    mn, mx = lax.fori_loop(0, nk, p1, (jnp.full((1, t), inf, F32), jnp.full((1, t), -inf, F32)))

    @pl.when(nk % 2 == 1)
    def _():
        I_ref[nk] = jnp.full((t, t), -inf, F32)

    nk2 = (nk + 1) // 2
    row2 = lax.broadcasted_iota(jnp.int32, (2 * t, 1), 0)

    def count(pred):
        def body(j, s):
            blk = I_ref[pl.ds(2 * j, 2)].reshape(2 * t, t)
            hit = jnp.where(pred(blk, j), 1.0, 0.0)
            return s + jnp.sum(hit.reshape(2 * t // 32, 32, t), axis=0)
        s = lax.fori_loop(0, nk2, body, jnp.zeros((32, t), F32))
        return jnp.sum(s, axis=0, keepdims=True)

    def min_where(pred):
        def body(j, s):
            blk = I_ref[pl.ds(2 * j, 2)].reshape(2 * t, t)
            return jnp.minimum(s, jnp.min(jnp.where(pred(blk), blk, inf).reshape(2 * t // 32, 32, t), axis=0))
        s = lax.fori_loop(0, nk2, body, jnp.full((32, t), inf, F32))
        return jnp.min(s, axis=0, keepdims=True)

    done0 = jnp.where(qpos + 1 > topk, 0.0, 1.0)
    tau0 = jnp.full((1, t), -inf, F32)
    hi0 = mx + (jnp.abs(mx) + 1.0)

    def cond_a(st):
        it, _, _, _, done = st
        return jnp.logical_and(it < 40, jnp.min(done) < 0.5)

    def body_a(st):
        it, lo, hi, tau, done = st
        mid = lo + (hi - lo) * 0.5
        c = count(lambda blk, _: blk >= mid)
        hit = jnp.logical_and(c == kf, done < 0.5)
        tau = jnp.where(hit, mid, tau)
        done = jnp.where(hit, 1.0, done)
        ge = c >= kf
        return it + 1, jnp.where(ge, mid, lo), jnp.where(ge, hi, mid), tau, done

    _, lo, hi, tau, done = lax.while_loop(cond_a, body_a, (jnp.int32(0), mn, hi0, tau0, done0))

    def cond_b(st):
        it, _, _, _, done, _, _ = st
        return jnp.logical_and(it < 4096, jnp.min(done) < 0.5)

    def body_b(st):
        it, lo, hi, tau, done, tie, need = st
        vlo = min_where(lambda blk: blk >= lo)
        ngt = count(lambda blk, _: blk > vlo)
        is_tie = jnp.logical_and(ngt < kf, done < 0.5)
        tau = jnp.where(is_tie, vlo, tau)
        need = jnp.where(is_tie, kf - ngt, need)
        tie = jnp.where(is_tie, 1.0, tie)
        done = jnp.where(is_tie, 1.0, done)
        lo2 = min_where(lambda blk: blk > vlo)
        mid = lo2 + (hi - lo2) * 0.5
        c = count(lambda blk, _: blk >= mid)
        hit = jnp.logical_and(c == kf, done < 0.5)
        tau = jnp.where(hit, mid, tau)
        done = jnp.where(hit, 1.0, done)
        ge = c >= kf
        return it + 1, jnp.where(ge, mid, lo2), jnp.where(ge, hi, mid), tau, done, tie, need

    zeros = jnp.zeros((1, t), F32)
    _, _, _, tau, done, tie, need = lax.while_loop(
        cond_b, body_b, (jnp.int32(0), lo, hi, tau, done, zeros, zeros))

    nbits = int(seq_len).bit_length() + 1
    ntrip = jnp.where(jnp.max(tie) > 0.5, nbits, 0)

    def body_j(_, st):
        jlo, jhi = st
        jm = (jlo + jhi) >> 1
        c = count(lambda blk, jj: jnp.logical_and(blk == tau, (jj * 2 * t + row2) <= jm))
        ok = c >= need
        return jnp.where(ok, jlo, jm), jnp.where(ok, jm, jhi)

    _, jhi = lax.fori_loop(0, ntrip, body_j,
                           (jnp.full((1, t), -1, jnp.int32), jnp.full((1, t), seq_len - 1, jnp.int32)))
    jstar = jnp.where(tie > 0.5, jhi, -1)

    ones_rows = jnp.ones((16, t), _MXU)

    def selected(c):
        blk = I_ref[c]
        return jnp.logical_or(blk > tau, jnp.logical_and(blk == tau, (c * t + rowi) <= jstar))

    mb = (HEAD_DIM * att_scale) * (jnp.max(jnp.abs(qnw_ref[...]), axis=1, keepdims=True)
                                   * jnp.max(jnp.abs(knw_ref[...]), axis=1, keepdims=True))
    l_ref[...] = jnp.zeros(l_ref.shape, F32)
    acc_ref[...] = jnp.zeros(acc_ref.shape, F32)

    def p3_bounded(c, carry):
        bias_ref[0] = jnp.where(selected(c), -mb, NEG)
        ls = []
        for h in range(ATTN_HEADS):
            p = h // 2
            rows = slice(h * HEAD_DIM, (h + 1) * HEAD_DIM)
            s = _dot(k_ref[c, :, p * 128:(p + 1) * 128], qT_ref[h]) + bias_ref[0]
            pb = jnp.exp2(s).astype(_MXU)
            ls.append(_dot(ones_rows, pb)[0:1])
            acc_ref[rows, :] += _dot(vT_ref[c, rows, :], pb)
        l_ref[...] += jnp.concatenate(ls, axis=0)
        return carry

    lax.fori_loop(0, nk, p3_bounded, 0)

    underflow = jnp.logical_not(jnp.min(l_ref[...]) > 2.0 ** -60)

    def p3(c, carry):
        bias_ref[0] = jnp.where(selected(c), 0.0, NEG)
        mcs = []
        for h in range(ATTN_HEADS):
            p = h // 2
            s = _dot(k_ref[c, :, p * 128:(p + 1) * 128], qT_ref[h]) + bias_ref[0]
            s_ref[h] = s
            mcs.append(jnp.max(s, axis=0, keepdims=True))
        m_old = m_ref[...]
        m_new = jnp.maximum(m_old, jnp.concatenate(mcs, axis=0))
        alpha = jnp.exp2(m_old - m_new)
        m_ref[...] = m_new
        ls = []
        for h in range(ATTN_HEADS):
            rows = slice(h * HEAD_DIM, (h + 1) * HEAD_DIM)
            pb = jnp.exp2(s_ref[h] - m_new[h:h + 1]).astype(_MXU)
            ls.append(_dot(ones_rows, pb)[0:1])
            acc_ref[rows, :] = alpha[h:h + 1] * acc_ref[rows, :] + _dot(vT_ref[c, rows, :], pb)
        l_ref[...] = alpha * l_ref[...] + jnp.concatenate(ls, axis=0)
        return carry

    @pl.when(underflow)
    def _():
        m_ref[...] = jnp.full(m_ref.shape, NEG, F32)
        l_ref[...] = jnp.zeros(l_ref.shape, F32)
        acc_ref[...] = jnp.zeros(acc_ref.shape, F32)
        lax.fori_loop(0, nk, p3, 0)

    for h in range(ATTN_HEADS):
        acc_ref[h * HEAD_DIM:(h + 1) * HEAD_DIM, :] = acc_ref[h * HEAD_DIM:(h + 1) * HEAD_DIM, :] / l_ref[h:h + 1, :]
    yt = acc_ref[...]
    ms = jnp.mean(yt * yt, axis=0, keepdims=True)
    o_ref[...] = (yt * lax.rsqrt(ms + EPS) * anw_ref[...]).T.astype(o_ref.dtype)


def _dsa(qT, iqT, iwT, k4, vT4, ik4, anw_col, qnw, knw, *, B, L, topk, att_scale):
    t = TILE
    nl = L // t
    n = B * L
    return pl.pallas_call(
        functools.partial(_dsa_kernel, topk=topk, seq_len=L, att_scale=att_scale),
        grid=(B, nl),
        in_specs=[pl.BlockSpec((None, ATTN_HEADS, 128, t), lambda b, q: (b, 0, 0, q)),
                  pl.BlockSpec((None, IDX_HEADS, 128, t), lambda b, q: (b, 0, 0, q)),
                  pl.BlockSpec((None, IDX_HEADS, t), lambda b, q: (b, 0, q)),
                  pl.BlockSpec((None, nl, t, 512), lambda b, q: (b, 0, 0, 0), pipeline_mode=pl.Buffered(1)),
                  pl.BlockSpec((None, nl, 512, t), lambda b, q: (b, 0, 0, 0), pipeline_mode=pl.Buffered(1)),
                  pl.BlockSpec((None, nl, t, 128), lambda b, q: (b, 0, 0, 0), pipeline_mode=pl.Buffered(1)),
                  pl.BlockSpec((512, 1), lambda b, q: (0, 0)),
                  pl.BlockSpec((1, 512), lambda b, q: (0, 0)),
                  pl.BlockSpec((1, 512), lambda b, q: (0, 0))],
        out_specs=pl.BlockSpec((t, 512), lambda b, q: (b * nl + q, 0)),
        out_shape=jax.ShapeDtypeStruct((n, 512), _MXU),
        scratch_shapes=[pltpu.VMEM((nl, t, t), F32),
                        pltpu.VMEM((ATTN_HEADS, t), F32),
                        pltpu.VMEM((ATTN_HEADS, t), F32),
                        pltpu.VMEM((512, t), F32),
                        pltpu.VMEM((2, t, t), F32),
                        pltpu.VMEM((ATTN_HEADS, t, t), F32)],
        compiler_params=_cparams(("arbitrary", "arbitrary")),
        name="dsa",
    )(qT, iqT, iwT, k4, vT4, ik4, anw_col, qnw, knw)


def _mix_out_kernel(x_ref, ya_ref, yc_ref, mod_ref, woa_ref, woc_ref, n2_ref, wqT_ref, sk_ref,
                    x1_ref, h2T_ref, sT_ref):
    proj = _dot(ya_ref[...], woa_ref[...]) + _dot(yc_ref[...], woc_ref[...])
    x1 = x_ref[...] + mod_ref[2:3, :] * proj
    x1_ref[...] = x1
    ms = jnp.mean(x1 * x1, axis=-1, keepdims=True)
    h2 = x1 * lax.rsqrt(ms + EPS) * n2_ref[...] * (1.0 + mod_ref[4:5, :]) + mod_ref[3:4, :]
    h2t = h2.T.astype(_MXU)
    h2T_ref[...] = h2t
    qpt = _dot(wqT_ref[...], h2t).astype(_MXU)
    for hp in range(2 * PEER_HEADS):
        sT_ref[hp] = _dot(sk_ref[hp], qpt[hp * 128:(hp + 1) * 128])


def _mix_out(x2, ya, yc, mod3, woa, woc, norm2_w, wqT, sk, *, B, L):
    n, d = x2.shape
    t = TILE
    nl = L // t
    full = lambda shape: pl.BlockSpec(shape, lambda i: (0,) * len(shape))
    return pl.pallas_call(
        _mix_out_kernel,
        grid=(n // t,),
        in_specs=[pl.BlockSpec((t, d), lambda i: (i, 0)),
                  pl.BlockSpec((t, 512), lambda i: (i, 0)),
                  pl.BlockSpec((t, 512), lambda i: (i, 0)),
                  pl.BlockSpec((None, 6, d), lambda i: (i // nl, 0, 0)),
                  full(woa.shape), full(woc.shape), full((1, d)), full(wqT.shape), full(sk.shape)],
        out_specs=(pl.BlockSpec((t, d), lambda i: (i, 0)),
                   pl.BlockSpec((d, t), lambda i: (0, i)),
                   pl.BlockSpec((2 * PEER_HEADS, PEER_KEYS, t), lambda i: (0, 0, i))),
        out_shape=(jax.ShapeDtypeStruct((n, d), F32),
                   jax.ShapeDtypeStruct((d, n), _MXU),
                   jax.ShapeDtypeStruct((2 * PEER_HEADS, PEER_KEYS, n), F32)),
        compiler_params=_cparams(("arbitrary",)),
        name="mix_out",
    )(x2, ya, yc, mod3, woa, woc, norm2_w, wqT, sk)


def _extract_topk(vals, n_take, tie_break):
    r, t = vals.shape
    rowi = lax.broadcasted_iota(jnp.int32, (r, t), 0) if tie_break else None
    work = vals
    rank = jnp.full((r, t), 99.0, F32)
    taken = []
    for kk in range(n_take):
        m = jnp.max(work, axis=0, keepdims=True)
        sel = work == m
        if tie_break:
            sel = rowi == jnp.min(jnp.where(sel, rowi, r), axis=0, keepdims=True)
        rank = jnp.where(sel, float(kk), rank)
        work = jnp.where(sel, -float("inf"), work)
        taken.append(m)
    return taken, rank


_COMBO_B_LIMIT = {1: 8, 2: 5, 3: 4, 4: 3, 5: 2, 6: 2, 7: 2}


def _route_head(s1, s2, tie_break):
    kk = PEER_TOPK
    ninf = -float("inf")
    v1, rank1 = _extract_topk(s1, kk, tie_break)
    v2, rank2 = _extract_topk(s2, kk, tie_break)
    v1all = jnp.concatenate(v1, axis=0)
    v2all = jnp.concatenate(v2, axis=0)
    sub = lax.broadcasted_iota(jnp.int32, (8, s1.shape[1]), 0)
    pieces = [v1[0] + v2all]
    for a in range(1, 8):
        piece = v1[a] + v2all[0:8]
        if _COMBO_B_LIMIT[a] < 8:
            piece = jnp.where(sub < _COMBO_B_LIMIT[a], piece, ninf)
        pieces.append(piece)
    pieces.append(v1all[8:16] + v2[0])
    combo = jnp.concatenate(pieces, axis=0)
    _, crank = _extract_topk(combo, kk, True)
    selc = jnp.where(crank < float(kk), 1.0, 0.0)
    z = jnp.sum(selc * jnp.exp(combo - combo[0:1]), axis=0, keepdims=True)
    cnts = [jnp.sum(selc[0:16], axis=0, keepdims=True)]
    cnts += [jnp.sum(selc[8 + 8 * a:16 + 8 * a], axis=0, keepdims=True) for a in range(1, 8)]
    cnts += [selc[72 + a:73 + a] for a in range(8)]
    cnt1 = jnp.zeros_like(s1)
    for a in range(kk):
        cnt1 = cnt1 + jnp.where(rank1 == float(a), cnts[a], 0.0)
    f2 = jnp.where(rank2 < float(kk), jnp.exp(s2 - v2[0]), 0.0)
    e1 = jnp.where(rank1 < float(kk), jnp.exp(s1 - v1[0]), 0.0) / z
    n_ranked = (jnp.sum(jnp.where(rank1 < float(kk), 1.0, 0.0), axis=0, keepdims=True)
                + jnp.sum(jnp.where(rank2 < float(kk), 1.0, 0.0), axis=0, keepdims=True))
    return rank2, f2, cnt1, e1, n_ranked


def _route_kernel(sT_ref, r2_ref, f2_ref, c1_ref, e1_ref):
    def run(tie_break):
        worst = jnp.zeros((1, sT_ref.shape[2]), F32)
        for h in range(PEER_HEADS):
            rank2, f2, cnt1, e1, n_ranked = _route_head(sT_ref[2 * h], sT_ref[2 * h + 1], tie_break)
            r2_ref[h] = rank2.astype(r2_ref.dtype)
            f2_ref[h] = f2.astype(f2_ref.dtype)
            c1_ref[h] = cnt1
            e1_ref[h] = e1
            worst = jnp.maximum(worst, n_ranked)
        return jnp.max(worst)

    most_ranked = run(False)

    @pl.when(most_ranked > 2.0 * PEER_TOPK)
    def _():
        run(True)


def _route(sT):
    hp, nkeys, n = sT.shape
    t = 128
    spec = pl.BlockSpec((PEER_HEADS, nkeys, t), lambda i: (0, 0, i))
    shp = lambda dt: jax.ShapeDtypeStruct((PEER_HEADS, nkeys, n), dt)
    return pl.pallas_call(
        _route_kernel,
        grid=(n // t,),
        in_specs=[pl.BlockSpec((hp, nkeys, t), lambda i: (0, 0, i))],
        out_specs=(spec, spec, spec, spec),
        out_shape=(shp(_MXU), shp(_MXU), shp(F32), shp(F32)),
        compiler_params=_cparams(("arbitrary",)),
        name="route",
    )(sT)


PEER_TM = 512
PEER_TE = 1024


def _gelu(a):
    return 0.5 * a * (1.0 + jnp.tanh(0.7978845608028654 * (a + 0.044715 * (a * a * a))))


def _peer_kernel(h2T_ref, r2_ref, f2_ref, c1_ref, e1_ref, u_ref, vT_ref, x1_ref, mod_ref, o_ref,
                 acc_ref, g_ref):
    e = pl.program_id(1)
    sub = PEER_TE // PEER_KEYS

    @pl.when(e == 0)
    def _():
        acc_ref[...] = jnp.zeros_like(acc_ref)

    a_all = _dot(u_ref[...], h2T_ref[...])
    for ii in range(sub):
        i = e * sub + ii
        w = jnp.zeros((PEER_KEYS, PEER_TM), _MXU)
        for h in range(PEER_HEADS):
            c1 = c1_ref[h, pl.ds(i, 1), :].astype(_MXU)
            e1 = e1_ref[h, pl.ds(i, 1), :].astype(_MXU)
            w = w + jnp.where(r2_ref[h] < c1, f2_ref[h], jnp.zeros((), _MXU)) * e1
        g = _gelu(a_all[ii * PEER_KEYS:(ii + 1) * PEER_KEYS]) * w.astype(F32)
        g_ref[ii * PEER_KEYS:(ii + 1) * PEER_KEYS, :] = g.astype(_MXU)
    acc_ref[...] += _dot(vT_ref[...], g_ref[...])

    @pl.when(e == pl.num_programs(1) - 1)
    def _():
        o_ref[...] = x1_ref[...] + mod_ref[5:6, :] * acc_ref[...].T


def _peer(h2T, r2, f2, c1, e1, u_b, vT_b, x1, mod3, *, B, L):
    d, n = h2T.shape
    ne = u_b.shape[0]
    tm, te = PEER_TM, PEER_TE
    ntl = L // tm
    rspec = pl.BlockSpec((PEER_HEADS, PEER_KEYS, tm), lambda i, e: (0, 0, i))
    return pl.pallas_call(
        _peer_kernel,
        grid=(n // tm, ne // te),
        in_specs=[pl.BlockSpec((d, tm), lambda i, e: (0, i)),
                  rspec, rspec, rspec, rspec,
                  pl.BlockSpec((te, d), lambda i, e: (e, 0)),
                  pl.BlockSpec((d, te), lambda i, e: (0, e)),
                  pl.BlockSpec((tm, d), lambda i, e: (i, 0)),
                  pl.BlockSpec((None, 6, d), lambda i, e: (i // ntl, 0, 0))],
        out_specs=pl.BlockSpec((tm, d), lambda i, e: (i, 0)),
        out_shape=jax.ShapeDtypeStruct((n, d), F32),
        scratch_shapes=[pltpu.VMEM((d, tm), F32), pltpu.VMEM((te, tm), _MXU)],
        compiler_params=_cparams(("arbitrary", "arbitrary")),
        name="peer",
    )(h2T, r2, f2, c1, e1, u_b, vT_b, x1, mod3)


def kernel(x, c, norm1_w, norm2_w, w_ada, b_ada, w_in, q_norm_w, k_norm_w, conv_w, conv_b,
           attn_out_norm_w, conv_out_norm_w, w_out, peer_wq, peer_subkeys, peer_u, peer_v):
    B, L, D = x.shape
    n = B * L
    assert D == 1024 and L % PEER_TM == 0 and L % (2 * TILE) == 0
    topk = min(MAX_TOPK, L // 4)
    att_scale = HEAD_DIM ** -0.5 * LOG2E
    idx_scale = (IDX_DIM ** -0.5) * (IDX_HEADS ** -0.5)

    c_pad = jnp.pad(c, ((0, 8 - B % 8 if B % 8 else 0), (0, 0)))
    w_ik = w_in[:, 3584:3648]
    w_iw = jnp.pad(w_in[:, 3648:3656], ((0, 0), (0, 120)))
    w_cat = jnp.concatenate([w_in[:, :3584], w_ik, w_ik, w_iw], axis=1).astype(_MXU)
    tile8 = lambda w: jnp.tile(w, ATTN_HEADS).reshape(1, 512)
    head_id = jnp.arange(512) // HEAD_DIM
    ones_bd = (head_id[:, None] == head_id[None, :]).astype(_MXU)
    woa = w_out[:512].astype(_MXU)
    woc = w_out[512:].astype(_MXU)
    wqT = peer_wq.T.astype(_MXU)
    sk = peer_subkeys.reshape(2 * PEER_HEADS, PEER_KEYS, PEER_KEYS).astype(_MXU)
    u_b = peer_u.astype(_MXU)
    vT_b = peer_v.T.astype(_MXU)

    mod = _ada(c_pad, w_ada, b_ada.reshape(1, -1))[:B]
    mod3 = mod.reshape(B, 6, D)
    x2 = x.reshape(n, D)

    qT, k, vT4, iqT, ik, iwT, yc = _mix_in(
        x2, mod3, norm1_w.reshape(1, D), w_cat, tile8(q_norm_w), tile8(k_norm_w),
        conv_w, conv_b.reshape(1, 512), conv_out_norm_w.reshape(1, 512), ones_bd,
        B=B, L=L, att_scale=att_scale, idx_scale=idx_scale)
    nl = L // TILE
    ya = _dsa(qT, iqT, iwT, k.reshape(B, nl, TILE, 512), vT4, ik.reshape(B, nl, TILE, 128),
              attn_out_norm_w.reshape(512, 1), tile8(q_norm_w), tile8(k_norm_w),
              B=B, L=L, topk=topk, att_scale=att_scale)
    x1, h2T, sT = _mix_out(x2, ya, yc, mod3, woa, woc, norm2_w.reshape(1, D), wqT, sk, B=B, L=L)
    r2, f2, c1, e1 = _route(sT)
    out = _peer(h2T, r2, f2, c1, e1, u_b, vT_b, x1, mod3, B=B, L=L)
    return out.reshape(B, L, D)
```

```python
import functools

import jax
import jax.numpy as jnp
from jax import lax
from jax.experimental import pallas as pl
from jax.experimental.pallas import tpu as pltpu

F32 = jnp.float32
_MXU = jnp.bfloat16

EPS = 1e-6
ATTN_HEADS = 8
HEAD_DIM = 64
IDX_HEADS = 8
IDX_DIM = 64
MAX_TOPK = 256
PEER_HEADS = 8
PEER_KEYS = 128
PEER_TOPK = 16
NEG = -1e30
LOG2E = 1.4426950408889634

TILE = 256
VMEM_LIMIT = 56 * 1024 * 1024


def _dot(a, b):
    return jnp.dot(a, b, preferred_element_type=F32)


def _cparams(sem, flags=None):
    return pltpu.CompilerParams(dimension_semantics=sem, vmem_limit_bytes=VMEM_LIMIT, flags=flags)


def _ada_kernel(c_ref, w_ref, b_ref, o_ref):
    c = c_ref[...]
    s = c * jax.nn.sigmoid(c)
    o_ref[...] = _dot(s.astype(_MXU), w_ref[...].astype(_MXU)) + b_ref[...]


def _ada(c_pad, w_ada, b_ada):
    rows, d = c_pad.shape
    n = w_ada.shape[1]
    tn = 1024
    return pl.pallas_call(
        _ada_kernel,
        grid=(n // tn,),
        in_specs=[pl.BlockSpec((rows, d), lambda j: (0, 0)),
                  pl.BlockSpec((d, tn), lambda j: (0, j)),
                  pl.BlockSpec((1, tn), lambda j: (0, j))],
        out_specs=pl.BlockSpec((rows, tn), lambda j: (0, j)),
        out_shape=jax.ShapeDtypeStruct((rows, n), F32),
        compiler_params=_cparams(("arbitrary",)),
        name="ada",
    )(c_pad, w_ada, b_ada)


def _split_pairs(zt, out_ref):
    t = zt.shape[1]
    row = lax.broadcasted_iota(jnp.int32, (128, t), 0)
    for p in range(4):
        pair = zt[p * 128:(p + 1) * 128]
        out_ref[2 * p] = jnp.where(row < 64, pair, 0.0).astype(out_ref.dtype)
        out_ref[2 * p + 1] = jnp.where(row >= 64, pair, 0.0).astype(out_ref.dtype)


def _mix_in_kernel(x_ref, mod_ref, n1_ref, w_ref, qnw_ref, knw_ref, cw_ref, cb_ref, cnw_ref, ones_ref,
                   qT_ref, k_ref, vT_ref, iqT_ref, ik_ref, iwT_ref, yc_ref, prev_ref,
                   *, att_scale, idx_scale):
    li = pl.program_id(1)
    x = x_ref[...]
    t = x.shape[0]
    ms = jnp.mean(x * x, axis=-1, keepdims=True)
    y = x * lax.rsqrt(ms + EPS) * n1_ref[...]
    h = (y * (1.0 + mod_ref[1:2, :]) + mod_ref[0:1, :]).astype(_MXU)

    def proj(c0, c1):
        return _dot(h, w_ref[:, c0:c1])

    ones = ones_ref[...]

    def head_norm(z, w):
        z2 = z * z
        hi = z2.astype(_MXU)
        lo = (z2 - hi.astype(F32)).astype(_MXU)
        ss = _dot(hi, ones) + _dot(lo, ones)
        return z * lax.rsqrt(ss * (1.0 / HEAD_DIM) + EPS) * w

    q = head_norm(proj(0, 512), qnw_ref[...]) * att_scale
    _split_pairs(q.T, qT_ref)
    k = head_norm(proj(512, 1024), knw_ref[...])
    k_ref[...] = k.astype(k_ref.dtype)
    vT_ref[...] = proj(1024, 1536).T.astype(vT_ref.dtype)
    _split_pairs(proj(3072, 3584).T, iqT_ref)
    tail = proj(3584, 3840)
    ik_ref[...] = tail[:, 0:128].astype(ik_ref.dtype)
    iwT_ref[...] = tail[:, 128:256].T[0:IDX_HEADS] * idx_scale

    cgate = proj(1536, 2048)
    u = proj(2048, 2560) * proj(2560, 3072)

    @pl.when(li == 0)
    def _():
        prev_ref[...] = jnp.zeros_like(prev_ref)

    prev = prev_ref[...]
    row = lax.broadcasted_iota(jnp.int32, (t, 1), 0)
    u1 = jnp.where(row == 0, prev[7:8], pltpu.roll(u, 1, 0))
    u2 = jnp.where(row == 0, prev[6:7], jnp.where(row == 1, prev[7:8], pltpu.roll(u, 2, 0)))
    prev_ref[...] = u[t - 8:t]
    yc = cgate * (cb_ref[...] + cw_ref[0:1] * u2 + cw_ref[1:2] * u1 + cw_ref[2:3] * u)
    msc = jnp.mean(yc * yc, axis=-1, keepdims=True)
    yc_ref[...] = (yc * lax.rsqrt(msc + EPS) * cnw_ref[...]).astype(yc_ref.dtype)


def _mix_in(x2, mod3, norm1_w, w_cat, qnw, knw, conv_w, conv_b, cnw, ones_bd, *, B, L, att_scale, idx_scale):
    n, d = x2.shape
    t = TILE
    nl = L // t
    full = lambda shape: pl.BlockSpec(shape, lambda b, l: (0,) * len(shape))
    tok = lambda w: pl.BlockSpec((t, w), lambda b, l: (b * nl + l, 0))
    out_shapes = (
        jax.ShapeDtypeStruct((B, ATTN_HEADS, 128, L), _MXU),
        jax.ShapeDtypeStruct((n, 512), _MXU),
        jax.ShapeDtypeStruct((B, nl, 512, t), _MXU),
        jax.ShapeDtypeStruct((B, IDX_HEADS, 128, L), _MXU),
        jax.ShapeDtypeStruct((n, 128), _MXU),
        jax.ShapeDtypeStruct((B, IDX_HEADS, L), F32),
        jax.ShapeDtypeStruct((n, 512), _MXU),
    )
    out_specs = (
        pl.BlockSpec((None, ATTN_HEADS, 128, t), lambda b, l: (b, 0, 0, l)),
        tok(512),
        pl.BlockSpec((None, None, 512, t), lambda b, l: (b, l, 0, 0)),
        pl.BlockSpec((None, IDX_HEADS, 128, t), lambda b, l: (b, 0, 0, l)),
        tok(128),
        pl.BlockSpec((None, IDX_HEADS, t), lambda b, l: (b, 0, l)),
        tok(512),
    )
    return pl.pallas_call(
        functools.partial(_mix_in_kernel, att_scale=att_scale, idx_scale=idx_scale),
        grid=(B, nl),
        in_specs=[tok(d),
                  pl.BlockSpec((None, 6, d), lambda b, l: (b, 0, 0)),
                  full((1, d)), full(w_cat.shape), full((1, 512)), full((1, 512)),
                  full((3, 512)), full((1, 512)), full((1, 512)), full((512, 512))],
        out_specs=out_specs,
        out_shape=out_shapes,
        scratch_shapes=[pltpu.VMEM((8, 512), F32)],
        compiler_params=_cparams(("arbitrary", "arbitrary")),
        name="mix_in",
    )(x2, mod3, norm1_w, w_cat, qnw, knw, conv_w, conv_b, cnw, ones_bd)


def _dsa_kernel(qT_ref, iqT_ref, iwT_ref, k_ref, vT_ref, ik_ref, anw_ref, qnw_ref, knw_ref, o_ref,
                I_ref, m_ref, l_ref, acc_ref, bias_ref, s_ref, *, topk, seq_len, att_scale):
    t = TILE
    qb = pl.program_id(1)
    nk = qb + 1
    kf = float(topk)
    inf = float("inf")
    qpos = qb * t + lax.broadcasted_iota(jnp.int32, (1, t), 1)
    rowi = lax.broadcasted_iota(jnp.int32, (t, 1), 0)

    def p1(c, carry):
        mn, mx = carry
        ikc = ik_ref[c]
        acc = jnp.zeros((t, t), F32)
        for h in range(IDX_HEADS):
            r = _dot(ikc, iqT_ref[h])
            acc = acc + iwT_ref[h:h + 1, :] * jnp.maximum(r, 0.0)
        causal = (c * t + rowi) <= qpos
        I_ref[c] = jnp.where(causal, acc, -inf)
        mn = jnp.minimum(mn, jnp.min(jnp.where(causal, acc, inf), axis=0, keepdims=True))
        mx = jnp.maximum(mx, jnp.max(jnp.where(causal, acc, -inf), axis=0, keepdims=True))
        return mn, mx

    mn, mx = lax.fori_loop(0, nk, p1, (jnp.full((1, t), inf, F32), jnp.full((1, t), -inf, F32)))

    @pl.when(nk % 2 == 1)
    def _():
        I_ref[nk] = jnp.full((t, t), -inf, F32)

    nk2 = (nk + 1) // 2
    row2 = lax.broadcasted_iota(jnp.int32, (2 * t, 1), 0)

    def count(pred):
        def body(j, s):
            blk = I_ref[pl.ds(2 * j, 2)].reshape(2 * t, t)
            hit = jnp.where(pred(blk, j), 1.0, 0.0)
            return s + jnp.sum(hit.reshape(2 * t // 32, 32, t), axis=0)
        s = lax.fori_loop(0, nk2, body, jnp.zeros((32, t), F32))
        return jnp.sum(s, axis=0, keepdims=True)

    def min_where(pred):
        def body(j, s):
            blk = I_ref[pl.ds(2 * j, 2)].reshape(2 * t, t)
            return jnp.minimum(s, jnp.min(jnp.where(pred(blk), blk, inf).reshape(2 * t // 32, 32, t), axis=0))
        s = lax.fori_loop(0, nk2, body, jnp.full((32, t), inf, F32))
        return jnp.min(s, axis=0, keepdims=True)

    done0 = jnp.where(qpos + 1 > topk, 0.0, 1.0)
    tau0 = jnp.full((1, t), -inf, F32)
    hi0 = mx + (jnp.abs(mx) + 1.0)

    def cond_a(st):
        it, _, _, _, done = st
        return jnp.logical_and(it < 40, jnp.min(done) < 0.5)

    def body_a(st):
        it, lo, hi, tau, done = st
        mid = lo + (hi - lo) * 0.5
        c = count(lambda blk, _: blk >= mid)
        hit = jnp.logical_and(c == kf, done < 0.5)
        tau = jnp.where(hit, mid, tau)
        done = jnp.where(hit, 1.0, done)
        ge = c >= kf
        return it + 1, jnp.where(ge, mid, lo), jnp.where(ge, hi, mid), tau, done

    _, lo, hi, tau, done = lax.while_loop(cond_a, body_a, (jnp.int32(0), mn, hi0, tau0, done0))

    def cond_b(st):
        it, _, _, _, done, _, _ = st
        return jnp.logical_and(it < 4096, jnp.min(done) < 0.5)

    def body_b(st):
        it, lo, hi, tau, done, tie, need = st
        vlo = min_where(lambda blk: blk >= lo)
        ngt = count(lambda blk, _: blk > vlo)
        is_tie = jnp.logical_and(ngt < kf, done < 0.5)
        tau = jnp.where(is_tie, vlo, tau)
        need = jnp.where(is_tie, kf - ngt, need)
        tie = jnp.where(is_tie, 1.0, tie)
        done = jnp.where(is_tie, 1.0, done)
        lo2 = min_where(lambda blk: blk > vlo)
        mid = lo2 + (hi - lo2) * 0.5
        c = count(lambda blk, _: blk >= mid)
        hit = jnp.logical_and(c == kf, done < 0.5)
        tau = jnp.where(hit, mid, tau)
        done = jnp.where(hit, 1.0, done)
        ge = c >= kf
        return it + 1, jnp.where(ge, mid, lo2), jnp.where(ge, hi, mid), tau, done, tie, need

    zeros = jnp.zeros((1, t), F32)
    _, _, _, tau, done, tie, need = lax.while_loop(
        cond_b, body_b, (jnp.int32(0), lo, hi, tau, done, zeros, zeros))

    nbits = int(seq_len).bit_length() + 1
    ntrip = jnp.where(jnp.max(tie) > 0.5, nbits, 0)

    def body_j(_, st):
        jlo, jhi = st
        jm = (jlo + jhi) >> 1
        c = count(lambda blk, jj: jnp.logical_and(blk == tau, (jj * 2 * t + row2) <= jm))
        ok = c >= need
        return jnp.where(ok, jlo, jm), jnp.where(ok, jm, jhi)

    _, jhi = lax.fori_loop(0, ntrip, body_j,
                           (jnp.full((1, t), -1, jnp.int32), jnp.full((1, t), seq_len - 1, jnp.int32)))
    jstar = jnp.where(tie > 0.5, jhi, -1)

    ones_rows = jnp.ones((16, t), _MXU)

    def selected(c):
        blk = I_ref[c]
        return jnp.logical_or(blk > tau, jnp.logical_and(blk == tau, (c * t + rowi) <= jstar))

    mb = (HEAD_DIM * att_scale) * (jnp.max(jnp.abs(qnw_ref[...]), axis=1, keepdims=True)
                                   * jnp.max(jnp.abs(knw_ref[...]), axis=1, keepdims=True))
    l_ref[...] = jnp.zeros(l_ref.shape, F32)
    acc_ref[...] = jnp.zeros(acc_ref.shape, F32)

    def p3_bounded(j, carry):
        for u in range(2):
            c = 2 * j + u
            bias_ref[u] = jnp.where(selected(c), -mb, NEG)
            ls = []
            for h in range(ATTN_HEADS):
                p = h // 2
                rows = slice(h * HEAD_DIM, (h + 1) * HEAD_DIM)
                s = _dot(k_ref[c, :, p * 128:(p + 1) * 128], qT_ref[h]) + bias_ref[u]
                pb = jnp.exp2(s).astype(_MXU)
                ls.append(_dot(ones_rows, pb)[0:1])
                acc_ref[rows, :] += _dot(vT_ref[c, rows, :], pb)
            l_ref[...] += jnp.concatenate(ls, axis=0)
        return carry

    lax.fori_loop(0, nk2, p3_bounded, 0)

    underflow = jnp.logical_not(jnp.min(l_ref[...]) > 2.0 ** -60)

    def p3(c, carry):
        bias_ref[0] = jnp.where(selected(c), 0.0, NEG)
        mcs = []
        for h in range(ATTN_HEADS):
            p = h // 2
            s = _dot(k_ref[c, :, p * 128:(p + 1) * 128], qT_ref[h]) + bias_ref[0]
            s_ref[h] = s
            mcs.append(jnp.max(s, axis=0, keepdims=True))
        m_old = m_ref[...]
        m_new = jnp.maximum(m_old, jnp.concatenate(mcs, axis=0))
        alpha = jnp.exp2(m_old - m_new)
        m_ref[...] = m_new
        ls = []
        for h in range(ATTN_HEADS):
            rows = slice(h * HEAD_DIM, (h + 1) * HEAD_DIM)
            pb = jnp.exp2(s_ref[h] - m_new[h:h + 1]).astype(_MXU)
            ls.append(_dot(ones_rows, pb)[0:1])
            acc_ref[rows, :] = alpha[h:h + 1] * acc_ref[rows, :] + _dot(vT_ref[c, rows, :], pb)
        l_ref[...] = alpha * l_ref[...] + jnp.concatenate(ls, axis=0)
        return carry

    @pl.when(underflow)
    def _():
        m_ref[...] = jnp.full(m_ref.shape, NEG, F32)
        l_ref[...] = jnp.zeros(l_ref.shape, F32)
        acc_ref[...] = jnp.zeros(acc_ref.shape, F32)
        lax.fori_loop(0, nk, p3, 0)

    for h in range(ATTN_HEADS):
        acc_ref[h * HEAD_DIM:(h + 1) * HEAD_DIM, :] = acc_ref[h * HEAD_DIM:(h + 1) * HEAD_DIM, :] / l_ref[h:h + 1, :]
    yt = acc_ref[...]
    ms = jnp.mean(yt * yt, axis=0, keepdims=True)
    o_ref[...] = (yt * lax.rsqrt(ms + EPS) * anw_ref[...]).T.astype(o_ref.dtype)


def _dsa(qT, iqT, iwT, k4, vT4, ik4, anw_col, qnw, knw, *, B, L, topk, att_scale):
    t = TILE
    nl = L // t
    n = B * L
    return pl.pallas_call(
        functools.partial(_dsa_kernel, topk=topk, seq_len=L, att_scale=att_scale),
        grid=(B, nl),
        in_specs=[pl.BlockSpec((None, ATTN_HEADS, 128, t), lambda b, q: (b, 0, 0, q)),
                  pl.BlockSpec((None, IDX_HEADS, 128, t), lambda b, q: (b, 0, 0, q)),
                  pl.BlockSpec((None, IDX_HEADS, t), lambda b, q: (b, 0, q)),
                  pl.BlockSpec((None, nl, t, 512), lambda b, q: (b, 0, 0, 0), pipeline_mode=pl.Buffered(1)),
                  pl.BlockSpec((None, nl, 512, t), lambda b, q: (b, 0, 0, 0), pipeline_mode=pl.Buffered(1)),
                  pl.BlockSpec((None, nl, t, 128), lambda b, q: (b, 0, 0, 0), pipeline_mode=pl.Buffered(1)),
                  pl.BlockSpec((512, 1), lambda b, q: (0, 0)),
                  pl.BlockSpec((1, 512), lambda b, q: (0, 0)),
                  pl.BlockSpec((1, 512), lambda b, q: (0, 0))],
        out_specs=pl.BlockSpec((t, 512), lambda b, q: (b * nl + q, 0)),
        out_shape=jax.ShapeDtypeStruct((n, 512), _MXU),
        scratch_shapes=[pltpu.VMEM((nl, t, t), F32),
                        pltpu.VMEM((ATTN_HEADS, t), F32),
                        pltpu.VMEM((ATTN_HEADS, t), F32),
                        pltpu.VMEM((512, t), F32),
                        pltpu.VMEM((2, t, t), F32),
                        pltpu.VMEM((ATTN_HEADS, t, t), F32)],
        compiler_params=_cparams(("arbitrary", "arbitrary")),
        name="dsa",
    )(qT, iqT, iwT, k4, vT4, ik4, anw_col, qnw, knw)


def _mix_out_kernel(x_ref, ya_ref, yc_ref, mod_ref, woa_ref, woc_ref, n2_ref, wqT_ref, sk_ref,
                    x1_ref, h2T_ref, sT_ref):
    proj = _dot(ya_ref[...], woa_ref[...]) + _dot(yc_ref[...], woc_ref[...])
    x1 = x_ref[...] + mod_ref[2:3, :] * proj
    x1_ref[...] = x1
    ms = jnp.mean(x1 * x1, axis=-1, keepdims=True)
    h2 = x1 * lax.rsqrt(ms + EPS) * n2_ref[...] * (1.0 + mod_ref[4:5, :]) + mod_ref[3:4, :]
    h2t = h2.T.astype(_MXU)
    h2T_ref[...] = h2t
    qpt = _dot(wqT_ref[...], h2t).astype(_MXU)
    for hp in range(2 * PEER_HEADS):
        sT_ref[hp] = _dot(sk_ref[hp], qpt[hp * 128:(hp + 1) * 128])


def _mix_out(x2, ya, yc, mod3, woa, woc, norm2_w, wqT, sk, *, B, L):
    n, d = x2.shape
    t = TILE
    nl = L // t
    full = lambda shape: pl.BlockSpec(shape, lambda i: (0,) * len(shape))
    return pl.pallas_call(
        _mix_out_kernel,
        grid=(n // t,),
        in_specs=[pl.BlockSpec((t, d), lambda i: (i, 0)),
                  pl.BlockSpec((t, 512), lambda i: (i, 0)),
                  pl.BlockSpec((t, 512), lambda i: (i, 0)),
                  pl.BlockSpec((None, 6, d), lambda i: (i // nl, 0, 0)),
                  full(woa.shape), full(woc.shape), full((1, d)), full(wqT.shape), full(sk.shape)],
        out_specs=(pl.BlockSpec((t, d), lambda i: (i, 0)),
                   pl.BlockSpec((d, t), lambda i: (0, i)),
                   pl.BlockSpec((2 * PEER_HEADS, PEER_KEYS, t), lambda i: (0, 0, i))),
        out_shape=(jax.ShapeDtypeStruct((n, d), F32),
                   jax.ShapeDtypeStruct((d, n), _MXU),
                   jax.ShapeDtypeStruct((2 * PEER_HEADS, PEER_KEYS, n), F32)),
        compiler_params=_cparams(("arbitrary",)),
        name="mix_out",
    )(x2, ya, yc, mod3, woa, woc, norm2_w, wqT, sk)


def _extract_topk(vals, n_take, tie_break):
    r, t = vals.shape
    rowi = lax.broadcasted_iota(jnp.int32, (r, t), 0) if tie_break else None
    work = vals
    rank = jnp.full((r, t), 99.0, F32)
    taken = []
    for kk in range(n_take):
        m = jnp.max(work, axis=0, keepdims=True)
        sel = work == m
        if tie_break:
            sel = rowi == jnp.min(jnp.where(sel, rowi, r), axis=0, keepdims=True)
        rank = jnp.where(sel, float(kk), rank)
        work = jnp.where(sel, -float("inf"), work)
        taken.append(m)
    return taken, rank


_COMBO_B_LIMIT = {1: 8, 2: 5, 3: 4, 4: 3, 5: 2, 6: 2, 7: 2}


def _route_head(s1, s2, tie_break):
    kk = PEER_TOPK
    ninf = -float("inf")
    v1, rank1 = _extract_topk(s1, kk, tie_break)
    v2, rank2 = _extract_topk(s2, kk, tie_break)
    v1all = jnp.concatenate(v1, axis=0)
    v2all = jnp.concatenate(v2, axis=0)
    sub = lax.broadcasted_iota(jnp.int32, (8, s1.shape[1]), 0)
    pieces = [v1[0] + v2all]
    for a in range(1, 8):
        piece = v1[a] + v2all[0:8]
        if _COMBO_B_LIMIT[a] < 8:
            piece = jnp.where(sub < _COMBO_B_LIMIT[a], piece, ninf)
        pieces.append(piece)
    pieces.append(v1all[8:16] + v2[0])
    combo = jnp.concatenate(pieces, axis=0)
    _, crank = _extract_topk(combo, kk, True)
    selc = jnp.where(crank < float(kk), 1.0, 0.0)
    z = jnp.sum(selc * jnp.exp(combo - combo[0:1]), axis=0, keepdims=True)
    cnts = [jnp.sum(selc[0:16], axis=0, keepdims=True)]
    cnts += [jnp.sum(selc[8 + 8 * a:16 + 8 * a], axis=0, keepdims=True) for a in range(1, 8)]
    cnts += [selc[72 + a:73 + a] for a in range(8)]
    cnt1 = jnp.zeros_like(s1)
    for a in range(kk):
        cnt1 = cnt1 + jnp.where(rank1 == float(a), cnts[a], 0.0)
    f2 = jnp.where(rank2 < float(kk), jnp.exp(s2 - v2[0]), 0.0)
    e1 = jnp.where(rank1 < float(kk), jnp.exp(s1 - v1[0]), 0.0) / z
    n_ranked = (jnp.sum(jnp.where(rank1 < float(kk), 1.0, 0.0), axis=0, keepdims=True)
                + jnp.sum(jnp.where(rank2 < float(kk), 1.0, 0.0), axis=0, keepdims=True))
    return rank2, f2, cnt1, e1, n_ranked


def _route_kernel(sT_ref, r2_ref, f2_ref, c1_ref, e1_ref):
    def run(tie_break):
        worst = jnp.zeros((1, sT_ref.shape[2]), F32)
        for h in range(PEER_HEADS):
            rank2, f2, cnt1, e1, n_ranked = _route_head(sT_ref[2 * h], sT_ref[2 * h + 1], tie_break)
            r2_ref[h] = rank2.astype(r2_ref.dtype)
            f2_ref[h] = f2.astype(f2_ref.dtype)
            c1_ref[h] = cnt1
            e1_ref[h] = e1
            worst = jnp.maximum(worst, n_ranked)
        return jnp.max(worst)

    most_ranked = run(False)

    @pl.when(most_ranked > 2.0 * PEER_TOPK)
    def _():
        run(True)


def _route(sT):
    hp, nkeys, n = sT.shape
    t = 128
    spec = pl.BlockSpec((PEER_HEADS, nkeys, t), lambda i: (0, 0, i))
    shp = lambda dt: jax.ShapeDtypeStruct((PEER_HEADS, nkeys, n), dt)
    return pl.pallas_call(
        _route_kernel,
        grid=(n // t,),
        in_specs=[pl.BlockSpec((hp, nkeys, t), lambda i: (0, 0, i))],
        out_specs=(spec, spec, spec, spec),
        out_shape=(shp(_MXU), shp(_MXU), shp(F32), shp(F32)),
        compiler_params=_cparams(("arbitrary",)),
        name="route",
    )(sT)


PEER_TM = 512
PEER_TE = 1024


def _gelu(a):
    c0 = 0.7978845608028654
    inner = a * (c0 + (c0 * 0.044715) * (a * a))
    return (0.5 * a) * (1.0 + jnp.tanh(inner))


def _peer_kernel(h2T_ref, r2_ref, f2_ref, c1_ref, e1_ref, u_ref, vT_ref, x1_ref, mod_ref, o_ref,
                 acc_ref, g_ref):
    e = pl.program_id(1)
    sub = PEER_TE // PEER_KEYS

    @pl.when(e == 0)
    def _():
        acc_ref[...] = jnp.zeros_like(acc_ref)

    a_all = _dot(u_ref[...], h2T_ref[...])
    for ii in range(sub):
        i = e * sub + ii
        w = jnp.zeros((PEER_KEYS, PEER_TM), _MXU)
        for h in range(PEER_HEADS):
            c1 = c1_ref[h, pl.ds(i, 1), :].astype(_MXU)
            e1 = e1_ref[h, pl.ds(i, 1), :].astype(_MXU)
            w = w + jnp.where(r2_ref[h] < c1, f2_ref[h], jnp.zeros((), _MXU)) * e1
        a = a_all[ii * PEER_KEYS:(ii + 1) * PEER_KEYS].astype(_MXU)
        g_ref[ii * PEER_KEYS:(ii + 1) * PEER_KEYS, :] = _gelu(a) * w
    acc_ref[...] += _dot(vT_ref[...], g_ref[...])

    @pl.when(e == pl.num_programs(1) - 1)
    def _():
        o_ref[...] = x1_ref[...] + mod_ref[5:6, :] * acc_ref[...].T


def _peer(h2T, r2, f2, c1, e1, u_b, vT_b, x1, mod3, *, B, L):
    d, n = h2T.shape
    ne = u_b.shape[0]
    tm, te = PEER_TM, PEER_TE
    ntl = L // tm
    rspec = pl.BlockSpec((PEER_HEADS, PEER_KEYS, tm), lambda i, e: (0, 0, i))
    return pl.pallas_call(
        _peer_kernel,
        grid=(n // tm, ne // te),
        in_specs=[pl.BlockSpec((d, tm), lambda i, e: (0, i)),
                  rspec, rspec, rspec, rspec,
                  pl.BlockSpec((te, d), lambda i, e: (e, 0)),
                  pl.BlockSpec((d, te), lambda i, e: (0, e)),
                  pl.BlockSpec((tm, d), lambda i, e: (i, 0)),
                  pl.BlockSpec((None, 6, d), lambda i, e: (i // ntl, 0, 0))],
        out_specs=pl.BlockSpec((tm, d), lambda i, e: (i, 0)),
        out_shape=jax.ShapeDtypeStruct((n, d), F32),
        scratch_shapes=[pltpu.VMEM((d, tm), F32), pltpu.VMEM((te, tm), _MXU)],
        compiler_params=_cparams(("arbitrary", "arbitrary")),
        name="peer",
    )(h2T, r2, f2, c1, e1, u_b, vT_b, x1, mod3)


def kernel(x, c, norm1_w, norm2_w, w_ada, b_ada, w_in, q_norm_w, k_norm_w, conv_w, conv_b,
           attn_out_norm_w, conv_out_norm_w, w_out, peer_wq, peer_subkeys, peer_u, peer_v):
    B, L, D = x.shape
    n = B * L
    assert D == 1024 and L % PEER_TM == 0 and L % (2 * TILE) == 0
    topk = min(MAX_TOPK, L // 4)
    att_scale = HEAD_DIM ** -0.5 * LOG2E
    idx_scale = (IDX_DIM ** -0.5) * (IDX_HEADS ** -0.5)

    c_pad = jnp.pad(c, ((0, 8 - B % 8 if B % 8 else 0), (0, 0)))
    w_ik = w_in[:, 3584:3648]
    w_iw = jnp.pad(w_in[:, 3648:3656], ((0, 0), (0, 120)))
    w_cat = jnp.concatenate([w_in[:, :3584], w_ik, w_ik, w_iw], axis=1).astype(_MXU)
    tile8 = lambda w: jnp.tile(w, ATTN_HEADS).reshape(1, 512)
    head_id = jnp.arange(512) // HEAD_DIM
    ones_bd = (head_id[:, None] == head_id[None, :]).astype(_MXU)
    woa = w_out[:512].astype(_MXU)
    woc = w_out[512:].astype(_MXU)
    wqT = peer_wq.T.astype(_MXU)
    sk = peer_subkeys.reshape(2 * PEER_HEADS, PEER_KEYS, PEER_KEYS).astype(_MXU)
    u_b = peer_u.astype(_MXU)
    vT_b = peer_v.T.astype(_MXU)

    mod = _ada(c_pad, w_ada, b_ada.reshape(1, -1))[:B]
    mod3 = mod.reshape(B, 6, D)
    x2 = x.reshape(n, D)

    qT, k, vT4, iqT, ik, iwT, yc = _mix_in(
        x2, mod3, norm1_w.reshape(1, D), w_cat, tile8(q_norm_w), tile8(k_norm_w),
        conv_w, conv_b.reshape(1, 512), conv_out_norm_w.reshape(1, 512), ones_bd,
        B=B, L=L, att_scale=att_scale, idx_scale=idx_scale)
    nl = L // TILE
    ya = _dsa(qT, iqT, iwT, k.reshape(B, nl, TILE, 512), vT4, ik.reshape(B, nl, TILE, 128),
              attn_out_norm_w.reshape(512, 1), tile8(q_norm_w), tile8(k_norm_w),
              B=B, L=L, topk=topk, att_scale=att_scale)
    x1, h2T, sT = _mix_out(x2, ya, yc, mod3, woa, woc, norm2_w.reshape(1, D), wqT, sk, B=B, L=L)
    r2, f2, c1, e1 = _route(sT)
    out = _peer(h2T, r2, f2, c1, e1, u_b, vT_b, x1, mod3, B=B, L=L)
    return out.reshape(B, L, D)
```

```python
import functools

import jax
import jax.numpy as jnp
from jax import lax
from jax.experimental import pallas as pl
from jax.experimental.pallas import tpu as pltpu

F32 = jnp.float32
_MXU = jnp.bfloat16

EPS = 1e-6
ATTN_HEADS = 8
HEAD_DIM = 64
IDX_HEADS = 8
IDX_DIM = 64
MAX_TOPK = 256
PEER_HEADS = 8
PEER_KEYS = 128
PEER_TOPK = 16
NEG = -1e30
LOG2E = 1.4426950408889634

TILE = 256
VMEM_LIMIT = 56 * 1024 * 1024


def _dot(a, b):
    return jnp.dot(a, b, preferred_element_type=F32)


def _cparams(sem, flags=None):
    return pltpu.CompilerParams(dimension_semantics=sem, vmem_limit_bytes=VMEM_LIMIT, flags=flags)


def _ada_kernel(c_ref, w_ref, b_ref, o_ref):
    c = c_ref[...]
    s = c * jax.nn.sigmoid(c)
    o_ref[...] = _dot(s.astype(_MXU), w_ref[...].astype(_MXU)) + b_ref[...]


def _ada(c_pad, w_ada, b_ada):
    rows, d = c_pad.shape
    n = w_ada.shape[1]
    tn = 1024
    return pl.pallas_call(
        _ada_kernel,
        grid=(n // tn,),
        in_specs=[pl.BlockSpec((rows, d), lambda j: (0, 0)),
                  pl.BlockSpec((d, tn), lambda j: (0, j)),
                  pl.BlockSpec((1, tn), lambda j: (0, j))],
        out_specs=pl.BlockSpec((rows, tn), lambda j: (0, j)),
        out_shape=jax.ShapeDtypeStruct((rows, n), F32),
        compiler_params=_cparams(("arbitrary",)),
        name="ada",
    )(c_pad, w_ada, b_ada)


def _split_pairs(zt, out_ref):
    t = zt.shape[1]
    row = lax.broadcasted_iota(jnp.int32, (128, t), 0)
    for p in range(4):
        pair = zt[p * 128:(p + 1) * 128]
        out_ref[2 * p] = jnp.where(row < 64, pair, 0.0).astype(out_ref.dtype)
        out_ref[2 * p + 1] = jnp.where(row >= 64, pair, 0.0).astype(out_ref.dtype)


def _mix_in_kernel(x_ref, mod_ref, n1_ref, w_ref, qnw_ref, knw_ref, cw_ref, cb_ref, cnw_ref, ones_ref,
                   qT_ref, k_ref, vT_ref, iqT_ref, ik_ref, iwT_ref, yc_ref, prev_ref,
                   *, att_scale, idx_scale):
    li = pl.program_id(1)
    x = x_ref[...]
    t = x.shape[0]
    ms = jnp.mean(x * x, axis=-1, keepdims=True)
    y = x * lax.rsqrt(ms + EPS) * n1_ref[...]
    h = (y * (1.0 + mod_ref[1:2, :]) + mod_ref[0:1, :]).astype(_MXU)

    def proj(c0, c1):
        return _dot(h, w_ref[:, c0:c1])

    ones = ones_ref[...]

    def head_norm(z, w):
        z2 = z * z
        hi = z2.astype(_MXU)
        lo = (z2 - hi.astype(F32)).astype(_MXU)
        ss = _dot(hi, ones) + _dot(lo, ones)
        return z * lax.rsqrt(ss * (1.0 / HEAD_DIM) + EPS) * w

    q = head_norm(proj(0, 512), qnw_ref[...]) * att_scale
    _split_pairs(q.T, qT_ref)
    k = head_norm(proj(512, 1024), knw_ref[...])
    k_ref[...] = k.astype(k_ref.dtype)
    vT_ref[...] = proj(1024, 1536).T.astype(vT_ref.dtype)
    _split_pairs(proj(3072, 3584).T, iqT_ref)
    tail = proj(3584, 3840)
    ik_ref[...] = tail[:, 0:128].astype(ik_ref.dtype)
    iwT_ref[...] = tail[:, 128:256].T[0:IDX_HEADS] * idx_scale

    cgate = proj(1536, 2048)
    u = proj(2048, 2560) * proj(2560, 3072)

    @pl.when(li == 0)
    def _():
        prev_ref[...] = jnp.zeros_like(prev_ref)

    prev = prev_ref[...]
    row = lax.broadcasted_iota(jnp.int32, (t, 1), 0)
    u1 = jnp.where(row == 0, prev[7:8], pltpu.roll(u, 1, 0))
    u2 = jnp.where(row == 0, prev[6:7], jnp.where(row == 1, prev[7:8], pltpu.roll(u, 2, 0)))
    prev_ref[...] = u[t - 8:t]
    yc = cgate * (cb_ref[...] + cw_ref[0:1] * u2 + cw_ref[1:2] * u1 + cw_ref[2:3] * u)
    msc = jnp.mean(yc * yc, axis=-1, keepdims=True)
    yc_ref[...] = (yc * lax.rsqrt(msc + EPS) * cnw_ref[...]).astype(yc_ref.dtype)


def _mix_in(x2, mod3, norm1_w, w_cat, qnw, knw, conv_w, conv_b, cnw, ones_bd, *, B, L, att_scale, idx_scale):
    n, d = x2.shape
    t = TILE
    nl = L // t
    full = lambda shape: pl.BlockSpec(shape, lambda b, l: (0,) * len(shape))
    tok = lambda w: pl.BlockSpec((t, w), lambda b, l: (b * nl + l, 0))
    out_shapes = (
        jax.ShapeDtypeStruct((B, ATTN_HEADS, 128, L), _MXU),
        jax.ShapeDtypeStruct((n, 512), _MXU),
        jax.ShapeDtypeStruct((B, nl, 512, t), _MXU),
        jax.ShapeDtypeStruct((B, IDX_HEADS, 128, L), _MXU),
        jax.ShapeDtypeStruct((n, 128), _MXU),
        jax.ShapeDtypeStruct((B, IDX_HEADS, L), F32),
        jax.ShapeDtypeStruct((n, 512), _MXU),
    )
    out_specs = (
        pl.BlockSpec((None, ATTN_HEADS, 128, t), lambda b, l: (b, 0, 0, l)),
        tok(512),
        pl.BlockSpec((None, None, 512, t), lambda b, l: (b, l, 0, 0)),
        pl.BlockSpec((None, IDX_HEADS, 128, t), lambda b, l: (b, 0, 0, l)),
        tok(128),
        pl.BlockSpec((None, IDX_HEADS, t), lambda b, l: (b, 0, l)),
        tok(512),
    )
    return pl.pallas_call(
        functools.partial(_mix_in_kernel, att_scale=att_scale, idx_scale=idx_scale),
        grid=(B, nl),
        in_specs=[tok(d),
                  pl.BlockSpec((None, 6, d), lambda b, l: (b, 0, 0)),
                  full((1, d)), full(w_cat.shape), full((1, 512)), full((1, 512)),
                  full((3, 512)), full((1, 512)), full((1, 512)), full((512, 512))],
        out_specs=out_specs,
        out_shape=out_shapes,
        scratch_shapes=[pltpu.VMEM((8, 512), F32)],
        compiler_params=_cparams(("arbitrary", "arbitrary")),
        name="mix_in",
    )(x2, mod3, norm1_w, w_cat, qnw, knw, conv_w, conv_b, cnw, ones_bd)


def _dsa_kernel(qT_ref, iqT_ref, iwT_ref, k_ref, vT_ref, ik_ref, anw_ref, qnw_ref, knw_ref, o_ref,
                I_ref, m_ref, l_ref, acc_ref, bias_ref, s_ref, *, topk, seq_len, att_scale):
    t = TILE
    qb = pl.program_id(1)
    nk = qb + 1
    kf = float(topk)
    inf = float("inf")
    qpos = qb * t + lax.broadcasted_iota(jnp.int32, (1, t), 1)
    rowi = lax.broadcasted_iota(jnp.int32, (t, 1), 0)

    def p1(c, carry):
        mn, mx = carry
        ikc = ik_ref[c]
        acc = jnp.zeros((t, t), F32)
        for h in range(IDX_HEADS):
            r = _dot(ikc, iqT_ref[h])
            acc = acc + iwT_ref[h:h + 1, :] * jnp.maximum(r, 0.0)
        causal = (c * t + rowi) <= qpos
        I_ref[c] = jnp.where(causal, acc, -inf)
        mn = jnp.minimum(mn, jnp.min(jnp.where(causal, acc, inf), axis=0, keepdims=True))
        mx = jnp.maximum(mx, jnp.max(jnp.where(causal, acc, -inf), axis=0, keepdims=True))
        return mn, mx

    mn, mx = lax.fori_loop(0, nk, p1, (jnp.full((1, t), inf, F32), jnp.full((1, t), -inf, F32)))

    @pl.when(nk % 2 == 1)
    def _():
        I_ref[nk] = jnp.full((t, t), -inf, F32)

    nk2 = (nk + 1) // 2

    slab = 32
    per_chunk = t // slab

    def scan_pairs(fn, init):
        def body(j, s):
            for m in range(2 * per_chunk):
                u, r = divmod(m, per_chunk)
                blk = I_ref[2 * j + u, r * slab:(r + 1) * slab, :]
                s = fn(s, blk, (2 * j + u) * t + r * slab)
            return s
        return lax.fori_loop(0, nk2, body, init)

    row_s = lax.broadcasted_iota(jnp.int32, (slab, 1), 0)

    def count(pred):
        s = scan_pairs(lambda s, blk, base: jnp.where(pred(blk, base + row_s), s + 1.0, s),
                       jnp.zeros((slab, t), F32))
        return jnp.sum(s, axis=0, keepdims=True)

    def min_where(pred):
        s = scan_pairs(lambda s, blk, base: jnp.minimum(s, jnp.where(pred(blk), blk, inf)),
                       jnp.full((slab, t), inf, F32))
        return jnp.min(s, axis=0, keepdims=True)
    done0 = jnp.where(qpos + 1 > topk, 0.0, 1.0)
    tau0 = jnp.full((1, t), -inf, F32)
    hi0 = mx + (jnp.abs(mx) + 1.0)

    def cond_a(st):
        it, _, _, _, done = st
        return jnp.logical_and(it < 40, jnp.min(done) < 0.5)

    def body_a(st):
        it, lo, hi, tau, done = st
        mid = lo + (hi - lo) * 0.5
        c = count(lambda blk, _: blk >= mid)
        hit = jnp.logical_and(c == kf, done < 0.5)
        tau = jnp.where(hit, mid, tau)
        done = jnp.where(hit, 1.0, done)
        ge = c >= kf
        return it + 1, jnp.where(ge, mid, lo), jnp.where(ge, hi, mid), tau, done

    _, lo, hi, tau, done = lax.while_loop(cond_a, body_a, (jnp.int32(0), mn, hi0, tau0, done0))

    def cond_b(st):
        it, _, _, _, done, _, _ = st
        return jnp.logical_and(it < 4096, jnp.min(done) < 0.5)

    def body_b(st):
        it, lo, hi, tau, done, tie, need = st
        vlo = min_where(lambda blk: blk >= lo)
        ngt = count(lambda blk, _: blk > vlo)
        is_tie = jnp.logical_and(ngt < kf, done < 0.5)
        tau = jnp.where(is_tie, vlo, tau)
        need = jnp.where(is_tie, kf - ngt, need)
        tie = jnp.where(is_tie, 1.0, tie)
        done = jnp.where(is_tie, 1.0, done)
        lo2 = min_where(lambda blk: blk > vlo)
        mid = lo2 + (hi - lo2) * 0.5
        c = count(lambda blk, _: blk >= mid)
        hit = jnp.logical_and(c == kf, done < 0.5)
        tau = jnp.where(hit, mid, tau)
        done = jnp.where(hit, 1.0, done)
        ge = c >= kf
        return it + 1, jnp.where(ge, mid, lo2), jnp.where(ge, hi, mid), tau, done, tie, need

    zeros = jnp.zeros((1, t), F32)
    _, _, _, tau, done, tie, need = lax.while_loop(
        cond_b, body_b, (jnp.int32(0), lo, hi, tau, done, zeros, zeros))

    nbits = int(seq_len).bit_length() + 1
    ntrip = jnp.where(jnp.max(tie) > 0.5, nbits, 0)

    def body_j(_, st):
        jlo, jhi = st
        jm = (jlo + jhi) >> 1
        c = count(lambda blk, kpos: jnp.logical_and(blk == tau, kpos <= jm))
        ok = c >= need
        return jnp.where(ok, jlo, jm), jnp.where(ok, jm, jhi)

    _, jhi = lax.fori_loop(0, ntrip, body_j,
                           (jnp.full((1, t), -1, jnp.int32), jnp.full((1, t), seq_len - 1, jnp.int32)))
    jstar = jnp.where(tie > 0.5, jhi, -1)

    ones_rows = jnp.ones((16, t), _MXU)

    def selected(c):
        blk = I_ref[c]
        return jnp.logical_or(blk > tau, jnp.logical_and(blk == tau, (c * t + rowi) <= jstar))

    mb = (HEAD_DIM * att_scale) * (jnp.max(jnp.abs(qnw_ref[...]), axis=1, keepdims=True)
                                   * jnp.max(jnp.abs(knw_ref[...]), axis=1, keepdims=True))
    l_ref[...] = jnp.zeros(l_ref.shape, F32)
    acc_ref[...] = jnp.zeros(acc_ref.shape, F32)

    def p3_bounded(j, carry):
        for u in range(2):
            c = 2 * j + u
            bias_ref[u] = jnp.where(selected(c), -mb, NEG)
            ls = []
            for h in range(ATTN_HEADS):
                p = h // 2
                rows = slice(h * HEAD_DIM, (h + 1) * HEAD_DIM)
                s = _dot(k_ref[c, :, p * 128:(p + 1) * 128], qT_ref[h]) + bias_ref[u]
                pb = jnp.exp2(s).astype(_MXU)
                ls.append(_dot(ones_rows, pb)[0:1])
                acc_ref[rows, :] += _dot(vT_ref[c, rows, :], pb)
            l_ref[...] += jnp.concatenate(ls, axis=0)
        return carry

    lax.fori_loop(0, nk2, p3_bounded, 0)

    underflow = jnp.logical_not(jnp.min(l_ref[...]) > 2.0 ** -60)

    def p3(c, carry):
        bias_ref[0] = jnp.where(selected(c), 0.0, NEG)
        mcs = []
        for h in range(ATTN_HEADS):
            p = h // 2
            s = _dot(k_ref[c, :, p * 128:(p + 1) * 128], qT_ref[h]) + bias_ref[0]
            s_ref[h] = s
            mcs.append(jnp.max(s, axis=0, keepdims=True))
        m_old = m_ref[...]
        m_new = jnp.maximum(m_old, jnp.concatenate(mcs, axis=0))
        alpha = jnp.exp2(m_old - m_new)
        m_ref[...] = m_new
        ls = []
        for h in range(ATTN_HEADS):
            rows = slice(h * HEAD_DIM, (h + 1) * HEAD_DIM)
            pb = jnp.exp2(s_ref[h] - m_new[h:h + 1]).astype(_MXU)
            ls.append(_dot(ones_rows, pb)[0:1])
            acc_ref[rows, :] = alpha[h:h + 1] * acc_ref[rows, :] + _dot(vT_ref[c, rows, :], pb)
        l_ref[...] = alpha * l_ref[...] + jnp.concatenate(ls, axis=0)
        return carry

    @pl.when(underflow)
    def _():
        m_ref[...] = jnp.full(m_ref.shape, NEG, F32)
        l_ref[...] = jnp.zeros(l_ref.shape, F32)
        acc_ref[...] = jnp.zeros(acc_ref.shape, F32)
        lax.fori_loop(0, nk, p3, 0)

    for h in range(ATTN_HEADS):
        acc_ref[h * HEAD_DIM:(h + 1) * HEAD_DIM, :] = acc_ref[h * HEAD_DIM:(h + 1) * HEAD_DIM, :] / l_ref[h:h + 1, :]
    yt = acc_ref[...]
    ms = jnp.mean(yt * yt, axis=0, keepdims=True)
    o_ref[...] = (yt * lax.rsqrt(ms + EPS) * anw_ref[...]).T.astype(o_ref.dtype)


def _dsa(qT, iqT, iwT, k4, vT4, ik4, anw_col, qnw, knw, *, B, L, topk, att_scale):
    t = TILE
    nl = L // t
    n = B * L
    return pl.pallas_call(
        functools.partial(_dsa_kernel, topk=topk, seq_len=L, att_scale=att_scale),
        grid=(B, nl),
        in_specs=[pl.BlockSpec((None, ATTN_HEADS, 128, t), lambda b, q: (b, 0, 0, q)),
                  pl.BlockSpec((None, IDX_HEADS, 128, t), lambda b, q: (b, 0, 0, q)),
                  pl.BlockSpec((None, IDX_HEADS, t), lambda b, q: (b, 0, q)),
                  pl.BlockSpec((None, nl, t, 512), lambda b, q: (b, 0, 0, 0), pipeline_mode=pl.Buffered(1)),
                  pl.BlockSpec((None, nl, 512, t), lambda b, q: (b, 0, 0, 0), pipeline_mode=pl.Buffered(1)),
                  pl.BlockSpec((None, nl, t, 128), lambda b, q: (b, 0, 0, 0), pipeline_mode=pl.Buffered(1)),
                  pl.BlockSpec((512, 1), lambda b, q: (0, 0)),
                  pl.BlockSpec((1, 512), lambda b, q: (0, 0)),
                  pl.BlockSpec((1, 512), lambda b, q: (0, 0))],
        out_specs=pl.BlockSpec((t, 512), lambda b, q: (b * nl + q, 0)),
        out_shape=jax.ShapeDtypeStruct((n, 512), _MXU),
        scratch_shapes=[pltpu.VMEM((nl, t, t), F32),
                        pltpu.VMEM((ATTN_HEADS, t), F32),
                        pltpu.VMEM((ATTN_HEADS, t), F32),
                        pltpu.VMEM((512, t), F32),
                        pltpu.VMEM((2, t, t), F32),
                        pltpu.VMEM((ATTN_HEADS, t, t), F32)],
        compiler_params=_cparams(("arbitrary", "arbitrary")),
        name="dsa",
    )(qT, iqT, iwT, k4, vT4, ik4, anw_col, qnw, knw)


def _mix_out_kernel(x_ref, ya_ref, yc_ref, mod_ref, woa_ref, woc_ref, n2_ref, wqT_ref, sk_ref,
                    x1_ref, h2T_ref, sT_ref):
    proj = _dot(ya_ref[...], woa_ref[...]) + _dot(yc_ref[...], woc_ref[...])
    x1 = x_ref[...] + mod_ref[2:3, :] * proj
    x1_ref[...] = x1
    ms = jnp.mean(x1 * x1, axis=-1, keepdims=True)
    h2 = x1 * lax.rsqrt(ms + EPS) * n2_ref[...] * (1.0 + mod_ref[4:5, :]) + mod_ref[3:4, :]
    h2t = h2.T.astype(_MXU)
    h2T_ref[...] = h2t
    qpt = _dot(wqT_ref[...], h2t).astype(_MXU)
    for hp in range(2 * PEER_HEADS):
        sT_ref[hp] = _dot(sk_ref[hp], qpt[hp * 128:(hp + 1) * 128])


def _mix_out(x2, ya, yc, mod3, woa, woc, norm2_w, wqT, sk, *, B, L):
    n, d = x2.shape
    t = TILE
    nl = L // t
    full = lambda shape: pl.BlockSpec(shape, lambda i: (0,) * len(shape))
    return pl.pallas_call(
        _mix_out_kernel,
        grid=(n // t,),
        in_specs=[pl.BlockSpec((t, d), lambda i: (i, 0)),
                  pl.BlockSpec((t, 512), lambda i: (i, 0)),
                  pl.BlockSpec((t, 512), lambda i: (i, 0)),
                  pl.BlockSpec((None, 6, d), lambda i: (i // nl, 0, 0)),
                  full(woa.shape), full(woc.shape), full((1, d)), full(wqT.shape), full(sk.shape)],
        out_specs=(pl.BlockSpec((t, d), lambda i: (i, 0)),
                   pl.BlockSpec((d, t), lambda i: (0, i)),
                   pl.BlockSpec((2 * PEER_HEADS, PEER_KEYS, t), lambda i: (0, 0, i))),
        out_shape=(jax.ShapeDtypeStruct((n, d), F32),
                   jax.ShapeDtypeStruct((d, n), _MXU),
                   jax.ShapeDtypeStruct((2 * PEER_HEADS, PEER_KEYS, n), F32)),
        compiler_params=_cparams(("arbitrary",)),
        name="mix_out",
    )(x2, ya, yc, mod3, woa, woc, norm2_w, wqT, sk)


def _extract_topk(vals, n_take, tie_break):
    r, t = vals.shape
    rowi = lax.broadcasted_iota(jnp.int32, (r, t), 0) if tie_break else None
    work = vals
    rank = jnp.full((r, t), 99.0, F32)
    taken = []
    for kk in range(n_take):
        m = jnp.max(work, axis=0, keepdims=True)
        sel = work == m
        if tie_break:
            sel = rowi == jnp.min(jnp.where(sel, rowi, r), axis=0, keepdims=True)
        rank = jnp.where(sel, float(kk), rank)
        work = jnp.where(sel, -float("inf"), work)
        taken.append(m)
    return taken, rank


_COMBO_B_LIMIT = {1: 8, 2: 5, 3: 4, 4: 3, 5: 2, 6: 2, 7: 2}


def _route_head(s1, s2, tie_break):
    kk = PEER_TOPK
    ninf = -float("inf")
    v1, rank1 = _extract_topk(s1, kk, tie_break)
    v2, rank2 = _extract_topk(s2, kk, tie_break)
    v1all = jnp.concatenate(v1, axis=0)
    v2all = jnp.concatenate(v2, axis=0)
    sub = lax.broadcasted_iota(jnp.int32, (8, s1.shape[1]), 0)
    pieces = [v1[0] + v2all]
    for a in range(1, 8):
        piece = v1[a] + v2all[0:8]
        if _COMBO_B_LIMIT[a] < 8:
            piece = jnp.where(sub < _COMBO_B_LIMIT[a], piece, ninf)
        pieces.append(piece)
    pieces.append(v1all[8:16] + v2[0])
    combo = jnp.concatenate(pieces, axis=0)
    _, crank = _extract_topk(combo, kk, True)
    selc = jnp.where(crank < float(kk), 1.0, 0.0)
    z = jnp.sum(selc * jnp.exp(combo - combo[0:1]), axis=0, keepdims=True)
    cnts = [jnp.sum(selc[0:16], axis=0, keepdims=True)]
    cnts += [jnp.sum(selc[8 + 8 * a:16 + 8 * a], axis=0, keepdims=True) for a in range(1, 8)]
    cnts += [selc[72 + a:73 + a] for a in range(8)]
    cnt1 = jnp.zeros_like(s1)
    for a in range(kk):
        cnt1 = cnt1 + jnp.where(rank1 == float(a), cnts[a], 0.0)
    f2 = jnp.where(rank2 < float(kk), jnp.exp(s2 - v2[0]), 0.0)
    e1 = jnp.where(rank1 < float(kk), jnp.exp(s1 - v1[0]), 0.0) / z
    n_ranked = (jnp.sum(jnp.where(rank1 < float(kk), 1.0, 0.0), axis=0, keepdims=True)
                + jnp.sum(jnp.where(rank2 < float(kk), 1.0, 0.0), axis=0, keepdims=True))
    return rank2, f2, cnt1, e1, n_ranked


def _route_kernel(sT_ref, r2_ref, f2_ref, c1_ref, e1_ref):
    def run(tie_break):
        worst = jnp.zeros((1, sT_ref.shape[2]), F32)
        for h in range(PEER_HEADS):
            rank2, f2, cnt1, e1, n_ranked = _route_head(sT_ref[2 * h], sT_ref[2 * h + 1], tie_break)
            r2_ref[h] = rank2.astype(r2_ref.dtype)
            f2_ref[h] = f2.astype(f2_ref.dtype)
            c1_ref[h] = cnt1
            e1_ref[h] = e1
            worst = jnp.maximum(worst, n_ranked)
        return jnp.max(worst)

    most_ranked = run(False)

    @pl.when(most_ranked > 2.0 * PEER_TOPK)
    def _():
        run(True)


def _route(sT):
    hp, nkeys, n = sT.shape
    t = 128
    spec = pl.BlockSpec((PEER_HEADS, nkeys, t), lambda i: (0, 0, i))
    shp = lambda dt: jax.ShapeDtypeStruct((PEER_HEADS, nkeys, n), dt)
    return pl.pallas_call(
        _route_kernel,
        grid=(n // t,),
        in_specs=[pl.BlockSpec((hp, nkeys, t), lambda i: (0, 0, i))],
        out_specs=(spec, spec, spec, spec),
        out_shape=(shp(_MXU), shp(_MXU), shp(F32), shp(F32)),
        compiler_params=_cparams(("arbitrary",)),
        name="route",
    )(sT)


PEER_TM = 512
PEER_TE = 1024


def _gelu(a):
    c0 = 0.7978845608028654
    inner = a * (c0 + (c0 * 0.044715) * (a * a))
    return (0.5 * a) * (1.0 + jnp.tanh(inner))


def _peer_kernel(h2T_ref, r2_ref, f2_ref, c1_ref, e1_ref, u_ref, vT_ref, x1_ref, mod_ref, o_ref,
                 acc_ref, g_ref):
    e = pl.program_id(1)
    sub = PEER_TE // PEER_KEYS

    @pl.when(e == 0)
    def _():
        acc_ref[...] = jnp.zeros_like(acc_ref)

    a_all = _dot(u_ref[...], h2T_ref[...])
    for ii in range(sub):
        i = e * sub + ii
        w = jnp.zeros((PEER_KEYS, PEER_TM), _MXU)
        for h in range(PEER_HEADS):
            c1 = c1_ref[h, pl.ds(i, 1), :].astype(_MXU)
            e1 = e1_ref[h, pl.ds(i, 1), :].astype(_MXU)
            w = w + jnp.where(r2_ref[h] < c1, f2_ref[h], jnp.zeros((), _MXU)) * e1
        a = a_all[ii * PEER_KEYS:(ii + 1) * PEER_KEYS].astype(_MXU)
        g_ref[ii * PEER_KEYS:(ii + 1) * PEER_KEYS, :] = _gelu(a) * w
    acc_ref[...] += _dot(vT_ref[...], g_ref[...])

    @pl.when(e == pl.num_programs(1) - 1)
    def _():
        o_ref[...] = x1_ref[...] + mod_ref[5:6, :] * acc_ref[...].T


def _peer(h2T, r2, f2, c1, e1, u_b, vT_b, x1, mod3, *, B, L):
    d, n = h2T.shape
    ne = u_b.shape[0]
    tm, te = PEER_TM, PEER_TE
    ntl = L // tm
    rspec = pl.BlockSpec((PEER_HEADS, PEER_KEYS, tm), lambda i, e: (0, 0, i))
    return pl.pallas_call(
        _peer_kernel,
        grid=(n // tm, ne // te),
        in_specs=[pl.BlockSpec((d, tm), lambda i, e: (0, i)),
                  rspec, rspec, rspec, rspec,
                  pl.BlockSpec((te, d), lambda i, e: (e, 0)),
                  pl.BlockSpec((d, te), lambda i, e: (0, e)),
                  pl.BlockSpec((tm, d), lambda i, e: (i, 0)),
                  pl.BlockSpec((None, 6, d), lambda i, e: (i // ntl, 0, 0))],
        out_specs=pl.BlockSpec((tm, d), lambda i, e: (i, 0)),
        out_shape=jax.ShapeDtypeStruct((n, d), F32),
        scratch_shapes=[pltpu.VMEM((d, tm), F32), pltpu.VMEM((te, tm), _MXU)],
        compiler_params=_cparams(("arbitrary", "arbitrary")),
        name="peer",
    )(h2T, r2, f2, c1, e1, u_b, vT_b, x1, mod3)


def kernel(x, c, norm1_w, norm2_w, w_ada, b_ada, w_in, q_norm_w, k_norm_w, conv_w, conv_b,
           attn_out_norm_w, conv_out_norm_w, w_out, peer_wq, peer_subkeys, peer_u, peer_v):
    B, L, D = x.shape
    n = B * L
    assert D == 1024 and L % PEER_TM == 0 and L % (2 * TILE) == 0
    topk = min(MAX_TOPK, L // 4)
    att_scale = HEAD_DIM ** -0.5 * LOG2E
    idx_scale = (IDX_DIM ** -0.5) * (IDX_HEADS ** -0.5)

    c_pad = jnp.pad(c, ((0, 8 - B % 8 if B % 8 else 0), (0, 0)))
    w_ik = w_in[:, 3584:3648]
    w_iw = jnp.pad(w_in[:, 3648:3656], ((0, 0), (0, 120)))
    w_cat = jnp.concatenate([w_in[:, :3584], w_ik, w_ik, w_iw], axis=1).astype(_MXU)
    tile8 = lambda w: jnp.tile(w, ATTN_HEADS).reshape(1, 512)
    head_id = jnp.arange(512) // HEAD_DIM
    ones_bd = (head_id[:, None] == head_id[None, :]).astype(_MXU)
    woa = w_out[:512].astype(_MXU)
    woc = w_out[512:].astype(_MXU)
    wqT = peer_wq.T.astype(_MXU)
    sk = peer_subkeys.reshape(2 * PEER_HEADS, PEER_KEYS, PEER_KEYS).astype(_MXU)
    u_b = peer_u.astype(_MXU)
    vT_b = peer_v.T.astype(_MXU)

    mod = _ada(c_pad, w_ada, b_ada.reshape(1, -1))[:B]
    mod3 = mod.reshape(B, 6, D)
    x2 = x.reshape(n, D)

    qT, k, vT4, iqT, ik, iwT, yc = _mix_in(
        x2, mod3, norm1_w.reshape(1, D), w_cat, tile8(q_norm_w), tile8(k_norm_w),
        conv_w, conv_b.reshape(1, 512), conv_out_norm_w.reshape(1, 512), ones_bd,
        B=B, L=L, att_scale=att_scale, idx_scale=idx_scale)
    nl = L // TILE
    ya = _dsa(qT, iqT, iwT, k.reshape(B, nl, TILE, 512), vT4, ik.reshape(B, nl, TILE, 128),
              attn_out_norm_w.reshape(512, 1), tile8(q_norm_w), tile8(k_norm_w),
              B=B, L=L, topk=topk, att_scale=att_scale)
    x1, h2T, sT = _mix_out(x2, ya, yc, mod3, woa, woc, norm2_w.reshape(1, D), wqT, sk, B=B, L=L)
    r2, f2, c1, e1 = _route(sT)
    out = _peer(h2T, r2, f2, c1, e1, u_b, vT_b, x1, mod3, B=B, L=L)
    return out.reshape(B, L, D)
```

```python
import functools

import jax
import jax.numpy as jnp
from jax import lax
from jax.experimental import pallas as pl
from jax.experimental.pallas import tpu as pltpu

F32 = jnp.float32
_MXU = jnp.bfloat16

EPS = 1e-6
ATTN_HEADS = 8
HEAD_DIM = 64
IDX_HEADS = 8
IDX_DIM = 64
MAX_TOPK = 256
PEER_HEADS = 8
PEER_KEYS = 128
PEER_TOPK = 16
NEG = -1e30
LOG2E = 1.4426950408889634

TILE = 256
BISECT_BLIND_STEPS = 19
VMEM_LIMIT = 56 * 1024 * 1024


def _dot(a, b):
    return jnp.dot(a, b, preferred_element_type=F32)


def _cparams(sem, flags=None):
    return pltpu.CompilerParams(dimension_semantics=sem, vmem_limit_bytes=VMEM_LIMIT, flags=flags)


def _ada_kernel(c_ref, w_ref, b_ref, o_ref):
    c = c_ref[...]
    s = c * jax.nn.sigmoid(c)
    o_ref[...] = _dot(s.astype(_MXU), w_ref[...].astype(_MXU)) + b_ref[...]


def _ada(c_pad, w_ada, b_ada):
    rows, d = c_pad.shape
    n = w_ada.shape[1]
    tn = 1024
    return pl.pallas_call(
        _ada_kernel,
        grid=(n // tn,),
        in_specs=[pl.BlockSpec((rows, d), lambda j: (0, 0)),
                  pl.BlockSpec((d, tn), lambda j: (0, j)),
                  pl.BlockSpec((1, tn), lambda j: (0, j))],
        out_specs=pl.BlockSpec((rows, tn), lambda j: (0, j)),
        out_shape=jax.ShapeDtypeStruct((rows, n), F32),
        compiler_params=_cparams(("arbitrary",)),
        name="ada",
    )(c_pad, w_ada, b_ada)


def _split_pairs(zt, out_ref):
    t = zt.shape[1]
    row = lax.broadcasted_iota(jnp.int32, (128, t), 0)
    for p in range(4):
        pair = zt[p * 128:(p + 1) * 128]
        out_ref[2 * p] = jnp.where(row < 64, pair, 0.0).astype(out_ref.dtype)
        out_ref[2 * p + 1] = jnp.where(row >= 64, pair, 0.0).astype(out_ref.dtype)


def _mix_in_kernel(x_ref, mod_ref, n1_ref, w_ref, qnw_ref, knw_ref, cw_ref, cb_ref, cnw_ref, ones_ref,
                   qT_ref, k_ref, vT_ref, iqT_ref, ik_ref, iwT_ref, yc_ref, prev_ref,
                   *, att_scale, idx_scale):
    li = pl.program_id(1)
    x = x_ref[...]
    t = x.shape[0]
    ms = jnp.mean(x * x, axis=-1, keepdims=True)
    y = x * lax.rsqrt(ms + EPS) * n1_ref[...]
    h = (y * (1.0 + mod_ref[1:2, :]) + mod_ref[0:1, :]).astype(_MXU)

    def proj(c0, c1):
        return _dot(h, w_ref[:, c0:c1])

    ones = ones_ref[...]

    def head_norm(z, w):
        z2 = z * z
        hi = z2.astype(_MXU)
        lo = (z2 - hi.astype(F32)).astype(_MXU)
        ss = _dot(hi, ones) + _dot(lo, ones)
        return z * lax.rsqrt(ss * (1.0 / HEAD_DIM) + EPS) * w

    q = head_norm(proj(0, 512), qnw_ref[...]) * att_scale
    _split_pairs(q.T, qT_ref)
    k = head_norm(proj(512, 1024), knw_ref[...])
    k_ref[...] = k.astype(k_ref.dtype)
    vT_ref[...] = proj(1024, 1536).T.astype(vT_ref.dtype)
    _split_pairs(proj(3072, 3584).T, iqT_ref)
    tail = proj(3584, 3840)
    ik_ref[...] = tail[:, 0:128].astype(ik_ref.dtype)
    iwT_ref[...] = tail[:, 128:256].T[0:IDX_HEADS] * idx_scale

    cgate = proj(1536, 2048)
    u = proj(2048, 2560) * proj(2560, 3072)

    @pl.when(li == 0)
    def _():
        prev_ref[...] = jnp.zeros_like(prev_ref)

    prev = prev_ref[...]
    row = lax.broadcasted_iota(jnp.int32, (t, 1), 0)
    u1 = jnp.where(row == 0, prev[7:8], pltpu.roll(u, 1, 0))
    u2 = jnp.where(row == 0, prev[6:7], jnp.where(row == 1, prev[7:8], pltpu.roll(u, 2, 0)))
    prev_ref[...] = u[t - 8:t]
    yc = cgate * (cb_ref[...] + cw_ref[0:1] * u2 + cw_ref[1:2] * u1 + cw_ref[2:3] * u)
    msc = jnp.mean(yc * yc, axis=-1, keepdims=True)
    yc_ref[...] = (yc * lax.rsqrt(msc + EPS) * cnw_ref[...]).astype(yc_ref.dtype)


def _mix_in(x2, mod3, norm1_w, w_cat, qnw, knw, conv_w, conv_b, cnw, ones_bd, *, B, L, att_scale, idx_scale):
    n, d = x2.shape
    t = TILE
    nl = L // t
    full = lambda shape: pl.BlockSpec(shape, lambda b, l: (0,) * len(shape))
    tok = lambda w: pl.BlockSpec((t, w), lambda b, l: (b * nl + l, 0))
    out_shapes = (
        jax.ShapeDtypeStruct((B, ATTN_HEADS, 128, L), _MXU),
        jax.ShapeDtypeStruct((n, 512), _MXU),
        jax.ShapeDtypeStruct((B, nl, 512, t), _MXU),
        jax.ShapeDtypeStruct((B, IDX_HEADS, 128, L), _MXU),
        jax.ShapeDtypeStruct((n, 128), _MXU),
        jax.ShapeDtypeStruct((B, IDX_HEADS, L), F32),
        jax.ShapeDtypeStruct((n, 512), _MXU),
    )
    out_specs = (
        pl.BlockSpec((None, ATTN_HEADS, 128, t), lambda b, l: (b, 0, 0, l)),
        tok(512),
        pl.BlockSpec((None, None, 512, t), lambda b, l: (b, l, 0, 0)),
        pl.BlockSpec((None, IDX_HEADS, 128, t), lambda b, l: (b, 0, 0, l)),
        tok(128),
        pl.BlockSpec((None, IDX_HEADS, t), lambda b, l: (b, 0, l)),
        tok(512),
    )
    return pl.pallas_call(
        functools.partial(_mix_in_kernel, att_scale=att_scale, idx_scale=idx_scale),
        grid=(B, nl),
        in_specs=[tok(d),
                  pl.BlockSpec((None, 6, d), lambda b, l: (b, 0, 0)),
                  full((1, d)), full(w_cat.shape), full((1, 512)), full((1, 512)),
                  full((3, 512)), full((1, 512)), full((1, 512)), full((512, 512))],
        out_specs=out_specs,
        out_shape=out_shapes,
        scratch_shapes=[pltpu.VMEM((8, 512), F32)],
        compiler_params=_cparams(("arbitrary", "arbitrary")),
        name="mix_in",
    )(x2, mod3, norm1_w, w_cat, qnw, knw, conv_w, conv_b, cnw, ones_bd)


def _dsa_kernel(qT_ref, iqT_ref, iwT_ref, k_ref, vT_ref, ik_ref, anw_ref, qnw_ref, knw_ref, o_ref,
                I_ref, m_ref, l_ref, acc_ref, bias_ref, s_ref, *, topk, seq_len, att_scale):
    t = TILE
    qb = pl.program_id(1)
    nk = qb + 1
    kf = float(topk)
    inf = float("inf")
    qpos = qb * t + lax.broadcasted_iota(jnp.int32, (1, t), 1)
    rowi = lax.broadcasted_iota(jnp.int32, (t, 1), 0)

    def p1(c, carry):
        mn, mx = carry
        ikc = ik_ref[c]
        acc = jnp.zeros((t, t), F32)
        for h in range(IDX_HEADS):
            r = _dot(ikc, iqT_ref[h])
            acc = acc + iwT_ref[h:h + 1, :] * jnp.maximum(r, 0.0)
        causal = (c * t + rowi) <= qpos
        I_ref[c] = jnp.where(causal, acc, -inf)
        mn = jnp.minimum(mn, jnp.min(jnp.where(causal, acc, inf), axis=0, keepdims=True))
        mx = jnp.maximum(mx, jnp.max(jnp.where(causal, acc, -inf), axis=0, keepdims=True))
        return mn, mx

    mn, mx = lax.fori_loop(0, nk, p1, (jnp.full((1, t), inf, F32), jnp.full((1, t), -inf, F32)))

    @pl.when(nk % 2 == 1)
    def _():
        I_ref[nk] = jnp.full((t, t), -inf, F32)

    nk2 = (nk + 1) // 2

    slab = 32
    per_chunk = t // slab

    def scan_pairs(fn, init):
        def body(j, s):
            for m in range(2 * per_chunk):
                u, r = divmod(m, per_chunk)
                blk = I_ref[2 * j + u, r * slab:(r + 1) * slab, :]
                s = fn(s, blk, (2 * j + u) * t + r * slab)
            return s
        return lax.fori_loop(0, nk2, body, init)

    row_s = lax.broadcasted_iota(jnp.int32, (slab, 1), 0)

    def count(pred):
        s = scan_pairs(lambda s, blk, base: jnp.where(pred(blk, base + row_s), s + 1.0, s),
                       jnp.zeros((slab, t), F32))
        return jnp.sum(s, axis=0, keepdims=True)

    def min_where(pred):
        s = scan_pairs(lambda s, blk, base: jnp.minimum(s, jnp.where(pred(blk), blk, inf)),
                       jnp.full((slab, t), inf, F32))
        return jnp.min(s, axis=0, keepdims=True)
    done0 = jnp.where(qpos + 1 > topk, 0.0, 1.0)
    tau0 = jnp.full((1, t), -inf, F32)
    hi0 = mx + (jnp.abs(mx) + 1.0)

    def cond_a(st):
        it, _, _, _, done = st
        return jnp.logical_and(it < 40, jnp.min(done) < 0.5)

    def body_a(st):
        it, lo, hi, tau, done = st
        mid = lo + (hi - lo) * 0.5
        c = count(lambda blk, _: blk >= mid)
        hit = jnp.logical_and(c == kf, done < 0.5)
        tau = jnp.where(hit, mid, tau)
        done = jnp.where(hit, 1.0, done)
        ge = c >= kf
        return it + 1, jnp.where(ge, mid, lo), jnp.where(ge, hi, mid), tau, done

    n_blind = jnp.where(qb * t + t > topk, BISECT_BLIND_STEPS, 0)
    st = lax.fori_loop(0, n_blind, lambda _, s: body_a(s), (jnp.int32(0), mn, hi0, tau0, done0))
    _, lo, hi, tau, done = lax.while_loop(cond_a, body_a, st)

    def cond_b(st):
        it, _, _, _, done, _, _ = st
        return jnp.logical_and(it < 4096, jnp.min(done) < 0.5)

    def body_b(st):
        it, lo, hi, tau, done, tie, need = st
        vlo = min_where(lambda blk: blk >= lo)
        ngt = count(lambda blk, _: blk > vlo)
        is_tie = jnp.logical_and(ngt < kf, done < 0.5)
        tau = jnp.where(is_tie, vlo, tau)
        need = jnp.where(is_tie, kf - ngt, need)
        tie = jnp.where(is_tie, 1.0, tie)
        done = jnp.where(is_tie, 1.0, done)
        lo2 = min_where(lambda blk: blk > vlo)
        mid = lo2 + (hi - lo2) * 0.5
        c = count(lambda blk, _: blk >= mid)
        hit = jnp.logical_and(c == kf, done < 0.5)
        tau = jnp.where(hit, mid, tau)
        done = jnp.where(hit, 1.0, done)
        ge = c >= kf
        return it + 1, jnp.where(ge, mid, lo2), jnp.where(ge, hi, mid), tau, done, tie, need

    zeros = jnp.zeros((1, t), F32)
    _, _, _, tau, done, tie, need = lax.while_loop(
        cond_b, body_b, (jnp.int32(0), lo, hi, tau, done, zeros, zeros))

    nbits = int(seq_len).bit_length() + 1
    ntrip = jnp.where(jnp.max(tie) > 0.5, nbits, 0)

    def body_j(_, st):
        jlo, jhi = st
        jm = (jlo + jhi) >> 1
        c = count(lambda blk, kpos: jnp.logical_and(blk == tau, kpos <= jm))
        ok = c >= need
        return jnp.where(ok, jlo, jm), jnp.where(ok, jm, jhi)

    _, jhi = lax.fori_loop(0, ntrip, body_j,
                           (jnp.full((1, t), -1, jnp.int32), jnp.full((1, t), seq_len - 1, jnp.int32)))
    jstar = jnp.where(tie > 0.5, jhi, -1)

    ones_rows = jnp.ones((16, t), _MXU)

    def selected(c):
        blk = I_ref[c]
        return jnp.logical_or(blk > tau, jnp.logical_and(blk == tau, (c * t + rowi) <= jstar))

    mb = (HEAD_DIM * att_scale) * (jnp.max(jnp.abs(qnw_ref[...]), axis=1, keepdims=True)
                                   * jnp.max(jnp.abs(knw_ref[...]), axis=1, keepdims=True))
    l_ref[...] = jnp.zeros(l_ref.shape, F32)
    acc_ref[...] = jnp.zeros(acc_ref.shape, F32)

    def p3_bounded(j, carry):
        for u in range(2):
            c = 2 * j + u
            bias_ref[u] = jnp.where(selected(c), -mb, NEG)
            ls = []
            for h in range(ATTN_HEADS):
                p = h // 2
                rows = slice(h * HEAD_DIM, (h + 1) * HEAD_DIM)
                s = _dot(k_ref[c, :, p * 128:(p + 1) * 128], qT_ref[h]) + bias_ref[u]
                pb = jnp.exp2(s).astype(_MXU)
                ls.append(_dot(ones_rows, pb)[0:1])
                acc_ref[rows, :] += _dot(vT_ref[c, rows, :], pb)
            l_ref[...] += jnp.concatenate(ls, axis=0)
        return carry

    lax.fori_loop(0, nk2, p3_bounded, 0)

    underflow = jnp.logical_not(jnp.min(l_ref[...]) > 2.0 ** -60)

    def p3(c, carry):
        bias_ref[0] = jnp.where(selected(c), 0.0, NEG)
        mcs = []
        for h in range(ATTN_HEADS):
            p = h // 2
            s = _dot(k_ref[c, :, p * 128:(p + 1) * 128], qT_ref[h]) + bias_ref[0]
            s_ref[h] = s
            mcs.append(jnp.max(s, axis=0, keepdims=True))
        m_old = m_ref[...]
        m_new = jnp.maximum(m_old, jnp.concatenate(mcs, axis=0))
        alpha = jnp.exp2(m_old - m_new)
        m_ref[...] = m_new
        ls = []
        for h in range(ATTN_HEADS):
            rows = slice(h * HEAD_DIM, (h + 1) * HEAD_DIM)
            pb = jnp.exp2(s_ref[h] - m_new[h:h + 1]).astype(_MXU)
            ls.append(_dot(ones_rows, pb)[0:1])
            acc_ref[rows, :] = alpha[h:h + 1] * acc_ref[rows, :] + _dot(vT_ref[c, rows, :], pb)
        l_ref[...] = alpha * l_ref[...] + jnp.concatenate(ls, axis=0)
        return carry

    @pl.when(underflow)
    def _():
        m_ref[...] = jnp.full(m_ref.shape, NEG, F32)
        l_ref[...] = jnp.zeros(l_ref.shape, F32)
        acc_ref[...] = jnp.zeros(acc_ref.shape, F32)
        lax.fori_loop(0, nk, p3, 0)

    for h in range(ATTN_HEADS):
        acc_ref[h * HEAD_DIM:(h + 1) * HEAD_DIM, :] = acc_ref[h * HEAD_DIM:(h + 1) * HEAD_DIM, :] / l_ref[h:h + 1, :]
    yt = acc_ref[...]
    ms = jnp.mean(yt * yt, axis=0, keepdims=True)
    o_ref[...] = (yt * lax.rsqrt(ms + EPS) * anw_ref[...]).T.astype(o_ref.dtype)


def _dsa(qT, iqT, iwT, k4, vT4, ik4, anw_col, qnw, knw, *, B, L, topk, att_scale):
    t = TILE
    nl = L // t
    n = B * L
    return pl.pallas_call(
        functools.partial(_dsa_kernel, topk=topk, seq_len=L, att_scale=att_scale),
        grid=(B, nl),
        in_specs=[pl.BlockSpec((None, ATTN_HEADS, 128, t), lambda b, q: (b, 0, 0, q)),
                  pl.BlockSpec((None, IDX_HEADS, 128, t), lambda b, q: (b, 0, 0, q)),
                  pl.BlockSpec((None, IDX_HEADS, t), lambda b, q: (b, 0, q)),
                  pl.BlockSpec((None, nl, t, 512), lambda b, q: (b, 0, 0, 0), pipeline_mode=pl.Buffered(1)),
                  pl.BlockSpec((None, nl, 512, t), lambda b, q: (b, 0, 0, 0), pipeline_mode=pl.Buffered(1)),
                  pl.BlockSpec((None, nl, t, 128), lambda b, q: (b, 0, 0, 0), pipeline_mode=pl.Buffered(1)),
                  pl.BlockSpec((512, 1), lambda b, q: (0, 0)),
                  pl.BlockSpec((1, 512), lambda b, q: (0, 0)),
                  pl.BlockSpec((1, 512), lambda b, q: (0, 0))],
        out_specs=pl.BlockSpec((t, 512), lambda b, q: (b * nl + q, 0)),
        out_shape=jax.ShapeDtypeStruct((n, 512), _MXU),
        scratch_shapes=[pltpu.VMEM((nl, t, t), F32),
                        pltpu.VMEM((ATTN_HEADS, t), F32),
                        pltpu.VMEM((ATTN_HEADS, t), F32),
                        pltpu.VMEM((512, t), F32),
                        pltpu.VMEM((2, t, t), F32),
                        pltpu.VMEM((ATTN_HEADS, t, t), F32)],
        compiler_params=_cparams(("arbitrary", "arbitrary")),
        name="dsa",
    )(qT, iqT, iwT, k4, vT4, ik4, anw_col, qnw, knw)


def _mix_out_kernel(x_ref, ya_ref, yc_ref, mod_ref, woa_ref, woc_ref, n2_ref, wqT_ref, sk_ref,
                    x1_ref, h2T_ref, sT_ref):
    proj = _dot(ya_ref[...], woa_ref[...]) + _dot(yc_ref[...], woc_ref[...])
    x1 = x_ref[...] + mod_ref[2:3, :] * proj
    x1_ref[...] = x1
    ms = jnp.mean(x1 * x1, axis=-1, keepdims=True)
    h2 = x1 * lax.rsqrt(ms + EPS) * n2_ref[...] * (1.0 + mod_ref[4:5, :]) + mod_ref[3:4, :]
    h2t = h2.T.astype(_MXU)
    h2T_ref[...] = h2t
    qpt = _dot(wqT_ref[...], h2t).astype(_MXU)
    for hp in range(2 * PEER_HEADS):
        sT_ref[hp] = _dot(sk_ref[hp], qpt[hp * 128:(hp + 1) * 128])


def _mix_out(x2, ya, yc, mod3, woa, woc, norm2_w, wqT, sk, *, B, L):
    n, d = x2.shape
    t = TILE
    nl = L // t
    full = lambda shape: pl.BlockSpec(shape, lambda i: (0,) * len(shape))
    return pl.pallas_call(
        _mix_out_kernel,
        grid=(n // t,),
        in_specs=[pl.BlockSpec((t, d), lambda i: (i, 0)),
                  pl.BlockSpec((t, 512), lambda i: (i, 0)),
                  pl.BlockSpec((t, 512), lambda i: (i, 0)),
                  pl.BlockSpec((None, 6, d), lambda i: (i // nl, 0, 0)),
                  full(woa.shape), full(woc.shape), full((1, d)), full(wqT.shape), full(sk.shape)],
        out_specs=(pl.BlockSpec((t, d), lambda i: (i, 0)),
                   pl.BlockSpec((d, t), lambda i: (0, i)),
                   pl.BlockSpec((2 * PEER_HEADS, PEER_KEYS, t), lambda i: (0, 0, i))),
        out_shape=(jax.ShapeDtypeStruct((n, d), F32),
                   jax.ShapeDtypeStruct((d, n), _MXU),
                   jax.ShapeDtypeStruct((2 * PEER_HEADS, PEER_KEYS, n), F32)),
        compiler_params=_cparams(("arbitrary",)),
        name="mix_out",
    )(x2, ya, yc, mod3, woa, woc, norm2_w, wqT, sk)


def _extract_topk(vals, n_take, tie_break):
    r, t = vals.shape
    rowi = lax.broadcasted_iota(jnp.int32, (r, t), 0) if tie_break else None
    work = vals
    rank = jnp.full((r, t), 99.0, F32)
    taken = []
    for kk in range(n_take):
        m = jnp.max(work, axis=0, keepdims=True)
        sel = work == m
        if tie_break:
            sel = rowi == jnp.min(jnp.where(sel, rowi, r), axis=0, keepdims=True)
        rank = jnp.where(sel, float(kk), rank)
        work = jnp.where(sel, -float("inf"), work)
        taken.append(m)
    return taken, rank


_COMBO_B_LIMIT = {1: 8, 2: 5, 3: 4, 4: 3, 5: 2, 6: 2, 7: 2}


def _route_head(s1, s2, tie_break):
    kk = PEER_TOPK
    ninf = -float("inf")
    v1, rank1 = _extract_topk(s1, kk, tie_break)
    v2, rank2 = _extract_topk(s2, kk, tie_break)
    v1all = jnp.concatenate(v1, axis=0)
    v2all = jnp.concatenate(v2, axis=0)
    sub = lax.broadcasted_iota(jnp.int32, (8, s1.shape[1]), 0)
    pieces = [v1[0] + v2all]
    for a in range(1, 8):
        piece = v1[a] + v2all[0:8]
        if _COMBO_B_LIMIT[a] < 8:
            piece = jnp.where(sub < _COMBO_B_LIMIT[a], piece, ninf)
        pieces.append(piece)
    pieces.append(v1all[8:16] + v2[0])
    combo = jnp.concatenate(pieces, axis=0)
    _, crank = _extract_topk(combo, kk, True)
    selc = jnp.where(crank < float(kk), 1.0, 0.0)
    z = jnp.sum(selc * jnp.exp(combo - combo[0:1]), axis=0, keepdims=True)
    cnts = [jnp.sum(selc[0:16], axis=0, keepdims=True)]
    cnts += [jnp.sum(selc[8 + 8 * a:16 + 8 * a], axis=0, keepdims=True) for a in range(1, 8)]
    cnts += [selc[72 + a:73 + a] for a in range(8)]
    cnt1 = jnp.zeros_like(s1)
    for a in range(kk):
        cnt1 = cnt1 + jnp.where(rank1 == float(a), cnts[a], 0.0)
    f2 = jnp.where(rank2 < float(kk), jnp.exp(s2 - v2[0]), 0.0)
    e1 = jnp.where(rank1 < float(kk), jnp.exp(s1 - v1[0]), 0.0) / z
    n_ranked = (jnp.sum(jnp.where(rank1 < float(kk), 1.0, 0.0), axis=0, keepdims=True)
                + jnp.sum(jnp.where(rank2 < float(kk), 1.0, 0.0), axis=0, keepdims=True))
    return rank2, f2, cnt1, e1, n_ranked


def _route_kernel(sT_ref, r2_ref, f2_ref, c1_ref, e1_ref):
    def run(tie_break):
        worst = jnp.zeros((1, sT_ref.shape[2]), F32)
        for h in range(PEER_HEADS):
            rank2, f2, cnt1, e1, n_ranked = _route_head(sT_ref[2 * h], sT_ref[2 * h + 1], tie_break)
            r2_ref[h] = rank2.astype(r2_ref.dtype)
            f2_ref[h] = f2.astype(f2_ref.dtype)
            c1_ref[h] = cnt1
            e1_ref[h] = e1
            worst = jnp.maximum(worst, n_ranked)
        return jnp.max(worst)

    most_ranked = run(False)

    @pl.when(most_ranked > 2.0 * PEER_TOPK)
    def _():
        run(True)


def _route(sT):
    hp, nkeys, n = sT.shape
    t = 128
    spec = pl.BlockSpec((PEER_HEADS, nkeys, t), lambda i: (0, 0, i))
    shp = lambda dt: jax.ShapeDtypeStruct((PEER_HEADS, nkeys, n), dt)
    return pl.pallas_call(
        _route_kernel,
        grid=(n // t,),
        in_specs=[pl.BlockSpec((hp, nkeys, t), lambda i: (0, 0, i))],
        out_specs=(spec, spec, spec, spec),
        out_shape=(shp(_MXU), shp(_MXU), shp(F32), shp(F32)),
        compiler_params=_cparams(("arbitrary",)),
        name="route",
    )(sT)


PEER_TM = 512
PEER_TE = 1024


def _gelu(a):
    c0 = 0.7978845608028654
    inner = a * (c0 + (c0 * 0.044715) * (a * a))
    return (0.5 * a) * (1.0 + jnp.tanh(inner))


def _peer_kernel(h2T_ref, r2_ref, f2_ref, c1_ref, e1_ref, u_ref, vT_ref, x1_ref, mod_ref, o_ref,
                 acc_ref, g_ref):
    e = pl.program_id(1)
    sub = PEER_TE // PEER_KEYS

    @pl.when(e == 0)
    def _():
        acc_ref[...] = jnp.zeros_like(acc_ref)

    a_all = _dot(u_ref[...], h2T_ref[...])
    for ii in range(sub):
        i = e * sub + ii
        w = jnp.zeros((PEER_KEYS, PEER_TM), _MXU)
        for h in range(PEER_HEADS):
            c1 = c1_ref[h, pl.ds(i, 1), :].astype(_MXU)
            e1 = e1_ref[h, pl.ds(i, 1), :].astype(_MXU)
            w = w + jnp.where(r2_ref[h] < c1, f2_ref[h], jnp.zeros((), _MXU)) * e1
        a = a_all[ii * PEER_KEYS:(ii + 1) * PEER_KEYS].astype(_MXU)
        g_ref[ii * PEER_KEYS:(ii + 1) * PEER_KEYS, :] = _gelu(a) * w
    acc_ref[...] += _dot(vT_ref[...], g_ref[...])

    @pl.when(e == pl.num_programs(1) - 1)
    def _():
        o_ref[...] = x1_ref[...] + mod_ref[5:6, :] * acc_ref[...].T


def _peer(h2T, r2, f2, c1, e1, u_b, vT_b, x1, mod3, *, B, L):
    d, n = h2T.shape
    ne = u_b.shape[0]
    tm, te = PEER_TM, PEER_TE
    ntl = L // tm
    rspec = pl.BlockSpec((PEER_HEADS, PEER_KEYS, tm), lambda i, e: (0, 0, i))
    return pl.pallas_call(
        _peer_kernel,
        grid=(n // tm, ne // te),
        in_specs=[pl.BlockSpec((d, tm), lambda i, e: (0, i)),
                  rspec, rspec, rspec, rspec,
                  pl.BlockSpec((te, d), lambda i, e: (e, 0)),
                  pl.BlockSpec((d, te), lambda i, e: (0, e)),
                  pl.BlockSpec((tm, d), lambda i, e: (i, 0)),
                  pl.BlockSpec((None, 6, d), lambda i, e: (i // ntl, 0, 0))],
        out_specs=pl.BlockSpec((tm, d), lambda i, e: (i, 0)),
        out_shape=jax.ShapeDtypeStruct((n, d), F32),
        scratch_shapes=[pltpu.VMEM((d, tm), F32), pltpu.VMEM((te, tm), _MXU)],
        compiler_params=_cparams(("arbitrary", "arbitrary")),
        name="peer",
    )(h2T, r2, f2, c1, e1, u_b, vT_b, x1, mod3)


def kernel(x, c, norm1_w, norm2_w, w_ada, b_ada, w_in, q_norm_w, k_norm_w, conv_w, conv_b,
           attn_out_norm_w, conv_out_norm_w, w_out, peer_wq, peer_subkeys, peer_u, peer_v):
    B, L, D = x.shape
    n = B * L
    assert D == 1024 and L % PEER_TM == 0 and L % (2 * TILE) == 0
    topk = min(MAX_TOPK, L // 4)
    att_scale = HEAD_DIM ** -0.5 * LOG2E
    idx_scale = (IDX_DIM ** -0.5) * (IDX_HEADS ** -0.5)

    c_pad = jnp.pad(c, ((0, 8 - B % 8 if B % 8 else 0), (0, 0)))
    w_ik = w_in[:, 3584:3648]
    w_iw = jnp.pad(w_in[:, 3648:3656], ((0, 0), (0, 120)))
    w_cat = jnp.concatenate([w_in[:, :3584], w_ik, w_ik, w_iw], axis=1).astype(_MXU)
    tile8 = lambda w: jnp.tile(w, ATTN_HEADS).reshape(1, 512)
    head_id = jnp.arange(512) // HEAD_DIM
    ones_bd = (head_id[:, None] == head_id[None, :]).astype(_MXU)
    woa = w_out[:512].astype(_MXU)
    woc = w_out[512:].astype(_MXU)
    wqT = peer_wq.T.astype(_MXU)
    sk = peer_subkeys.reshape(2 * PEER_HEADS, PEER_KEYS, PEER_KEYS).astype(_MXU)
    u_b = peer_u.astype(_MXU)
    vT_b = peer_v.T.astype(_MXU)

    mod = _ada(c_pad, w_ada, b_ada.reshape(1, -1))[:B]
    mod3 = mod.reshape(B, 6, D)
    x2 = x.reshape(n, D)

    qT, k, vT4, iqT, ik, iwT, yc = _mix_in(
        x2, mod3, norm1_w.reshape(1, D), w_cat, tile8(q_norm_w), tile8(k_norm_w),
        conv_w, conv_b.reshape(1, 512), conv_out_norm_w.reshape(1, 512), ones_bd,
        B=B, L=L, att_scale=att_scale, idx_scale=idx_scale)
    nl = L // TILE
    ya = _dsa(qT, iqT, iwT, k.reshape(B, nl, TILE, 512), vT4, ik.reshape(B, nl, TILE, 128),
              attn_out_norm_w.reshape(512, 1), tile8(q_norm_w), tile8(k_norm_w),
              B=B, L=L, topk=topk, att_scale=att_scale)
    x1, h2T, sT = _mix_out(x2, ya, yc, mod3, woa, woc, norm2_w.reshape(1, D), wqT, sk, B=B, L=L)
    r2, f2, c1, e1 = _route(sT)
    out = _peer(h2T, r2, f2, c1, e1, u_b, vT_b, x1, mod3, B=B, L=L)
    return out.reshape(B, L, D)
```

```python
import functools

import jax
import jax.numpy as jnp
from jax import lax
from jax.experimental import pallas as pl
from jax.experimental.pallas import tpu as pltpu

F32 = jnp.float32
_MXU = jnp.bfloat16

EPS = 1e-6
ATTN_HEADS = 8
HEAD_DIM = 64
IDX_HEADS = 8
IDX_DIM = 64
MAX_TOPK = 256
PEER_HEADS = 8
PEER_KEYS = 128
PEER_TOPK = 16
NEG = -1e30
LOG2E = 1.4426950408889634

TILE = 256
BISECT_BLIND_STEPS = 19
VMEM_LIMIT = 56 * 1024 * 1024


def _dot(a, b):
    return jnp.dot(a, b, preferred_element_type=F32)


def _cparams(sem, flags=None):
    return pltpu.CompilerParams(dimension_semantics=sem, vmem_limit_bytes=VMEM_LIMIT, flags=flags)


def _ada_kernel(c_ref, w_ref, b_ref, o_ref):
    c = c_ref[...]
    s = c * jax.nn.sigmoid(c)
    o_ref[...] = _dot(s.astype(_MXU), w_ref[...].astype(_MXU)) + b_ref[...]


def _ada(c_pad, w_ada, b_ada):
    rows, d = c_pad.shape
    n = w_ada.shape[1]
    tn = 1024
    return pl.pallas_call(
        _ada_kernel,
        grid=(n // tn,),
        in_specs=[pl.BlockSpec((rows, d), lambda j: (0, 0)),
                  pl.BlockSpec((d, tn), lambda j: (0, j)),
                  pl.BlockSpec((1, tn), lambda j: (0, j))],
        out_specs=pl.BlockSpec((rows, tn), lambda j: (0, j)),
        out_shape=jax.ShapeDtypeStruct((rows, n), F32),
        compiler_params=_cparams(("arbitrary",)),
        name="ada",
    )(c_pad, w_ada, b_ada)


def _split_pairs(zt, out_ref):
    t = zt.shape[1]
    row = lax.broadcasted_iota(jnp.int32, (128, t), 0)
    for p in range(4):
        pair = zt[p * 128:(p + 1) * 128]
        out_ref[2 * p] = jnp.where(row < 64, pair, 0.0).astype(out_ref.dtype)
        out_ref[2 * p + 1] = jnp.where(row >= 64, pair, 0.0).astype(out_ref.dtype)


def _mix_in_kernel(x_ref, mod_ref, n1_ref, w_ref, qnw_ref, knw_ref, cw_ref, cb_ref, cnw_ref, ones_ref,
                   qT_ref, k_ref, vT_ref, iqT_ref, ik_ref, iwT_ref, yc_ref, prev_ref,
                   *, att_scale, idx_scale):
    li = pl.program_id(1)
    x = x_ref[...]
    t = x.shape[0]
    ms = jnp.mean(x * x, axis=-1, keepdims=True)
    y = x * lax.rsqrt(ms + EPS) * n1_ref[...]
    h = (y * (1.0 + mod_ref[1:2, :]) + mod_ref[0:1, :]).astype(_MXU)

    def proj(c0, c1):
        return _dot(h, w_ref[:, c0:c1])

    ones = ones_ref[...]

    def head_norm(z, w):
        z2 = z * z
        hi = z2.astype(_MXU)
        lo = (z2 - hi.astype(F32)).astype(_MXU)
        ss = _dot(hi, ones) + _dot(lo, ones)
        return z * lax.rsqrt(ss * (1.0 / HEAD_DIM) + EPS) * w

    q = head_norm(proj(0, 512), qnw_ref[...]) * att_scale
    _split_pairs(q.T, qT_ref)
    k = head_norm(proj(512, 1024), knw_ref[...])
    k_ref[...] = k.astype(k_ref.dtype)
    vT_ref[...] = proj(1024, 1536).T.astype(vT_ref.dtype)
    _split_pairs(proj(3072, 3584).T, iqT_ref)
    tail = proj(3584, 3840)
    ik_ref[...] = tail[:, 0:128].astype(ik_ref.dtype)
    iwT_ref[...] = tail[:, 128:256].T[0:IDX_HEADS] * idx_scale

    cgate = proj(1536, 2048)
    u = proj(2048, 2560) * proj(2560, 3072)

    @pl.when(li == 0)
    def _():
        prev_ref[...] = jnp.zeros_like(prev_ref)

    prev = prev_ref[...]
    row = lax.broadcasted_iota(jnp.int32, (t, 1), 0)
    u1 = jnp.where(row == 0, prev[7:8], pltpu.roll(u, 1, 0))
    u2 = jnp.where(row == 0, prev[6:7], jnp.where(row == 1, prev[7:8], pltpu.roll(u, 2, 0)))
    prev_ref[...] = u[t - 8:t]
    yc = cgate * (cb_ref[...] + cw_ref[0:1] * u2 + cw_ref[1:2] * u1 + cw_ref[2:3] * u)
    msc = jnp.mean(yc * yc, axis=-1, keepdims=True)
    yc_ref[...] = (yc * lax.rsqrt(msc + EPS) * cnw_ref[...]).astype(yc_ref.dtype)


def _mix_in(x2, mod3, norm1_w, w_cat, qnw, knw, conv_w, conv_b, cnw, ones_bd, *, B, L, att_scale, idx_scale):
    n, d = x2.shape
    t = TILE
    nl = L // t
    full = lambda shape: pl.BlockSpec(shape, lambda b, l: (0,) * len(shape))
    tok = lambda w: pl.BlockSpec((t, w), lambda b, l: (b * nl + l, 0))
    out_shapes = (
        jax.ShapeDtypeStruct((B, ATTN_HEADS, 128, L), _MXU),
        jax.ShapeDtypeStruct((n, 512), _MXU),
        jax.ShapeDtypeStruct((B, nl, 512, t), _MXU),
        jax.ShapeDtypeStruct((B, IDX_HEADS, 128, L), _MXU),
        jax.ShapeDtypeStruct((n, 128), _MXU),
        jax.ShapeDtypeStruct((B, IDX_HEADS, L), F32),
        jax.ShapeDtypeStruct((n, 512), _MXU),
    )
    out_specs = (
        pl.BlockSpec((None, ATTN_HEADS, 128, t), lambda b, l: (b, 0, 0, l)),
        tok(512),
        pl.BlockSpec((None, None, 512, t), lambda b, l: (b, l, 0, 0)),
        pl.BlockSpec((None, IDX_HEADS, 128, t), lambda b, l: (b, 0, 0, l)),
        tok(128),
        pl.BlockSpec((None, IDX_HEADS, t), lambda b, l: (b, 0, l)),
        tok(512),
    )
    return pl.pallas_call(
        functools.partial(_mix_in_kernel, att_scale=att_scale, idx_scale=idx_scale),
        grid=(B, nl),
        in_specs=[tok(d),
                  pl.BlockSpec((None, 6, d), lambda b, l: (b, 0, 0)),
                  full((1, d)), full(w_cat.shape), full((1, 512)), full((1, 512)),
                  full((3, 512)), full((1, 512)), full((1, 512)), full((512, 512))],
        out_specs=out_specs,
        out_shape=out_shapes,
        scratch_shapes=[pltpu.VMEM((8, 512), F32)],
        compiler_params=_cparams(("arbitrary", "arbitrary")),
        name="mix_in",
    )(x2, mod3, norm1_w, w_cat, qnw, knw, conv_w, conv_b, cnw, ones_bd)


def _dsa_kernel(qT_ref, iqT_ref, iwT_ref, k_ref, vT_ref, ik_ref, anw_ref, qnw_ref, knw_ref, o_ref,
                I_ref, m_ref, l_ref, acc_ref, bias_ref, s_ref, *, topk, seq_len, att_scale):
    t = TILE
    qb = pl.program_id(1)
    nk = qb + 1
    kf = float(topk)
    inf = float("inf")
    qpos = qb * t + lax.broadcasted_iota(jnp.int32, (1, t), 1)
    rowi = lax.broadcasted_iota(jnp.int32, (t, 1), 0)

    def scores(c):
        ikc = ik_ref[c]
        acc = jnp.zeros((t, t), F32)
        for h in range(IDX_HEADS):
            r = _dot(ikc, iqT_ref[h])
            acc = acc + iwT_ref[h:h + 1, :] * jnp.maximum(r, 0.0)
        return acc

    def p1(c, carry):
        mn, mx = carry
        acc = scores(c)
        I_ref[c] = acc
        return (jnp.minimum(mn, jnp.min(acc, axis=0, keepdims=True)),
                jnp.maximum(mx, jnp.max(acc, axis=0, keepdims=True)))

    mn, mx = lax.fori_loop(0, qb, p1, (jnp.full((1, t), inf, F32), jnp.full((1, t), -inf, F32)))
    acc = scores(qb)
    causal = (qb * t + rowi) <= qpos
    I_ref[qb] = jnp.where(causal, acc, -inf)
    mn = jnp.minimum(mn, jnp.min(jnp.where(causal, acc, inf), axis=0, keepdims=True))
    mx = jnp.maximum(mx, jnp.max(jnp.where(causal, acc, -inf), axis=0, keepdims=True))

    @pl.when(nk % 2 == 1)
    def _():
        I_ref[nk] = jnp.full((t, t), -inf, F32)

    nk2 = (nk + 1) // 2

    slab = 32
    per_chunk = t // slab

    def scan_pairs(fn, init):
        def body(j, s):
            for m in range(2 * per_chunk):
                u, r = divmod(m, per_chunk)
                blk = I_ref[2 * j + u, r * slab:(r + 1) * slab, :]
                s = fn(s, blk, (2 * j + u) * t + r * slab)
            return s
        return lax.fori_loop(0, nk2, body, init)

    row_s = lax.broadcasted_iota(jnp.int32, (slab, 1), 0)

    def count(pred):
        s = scan_pairs(lambda s, blk, base: jnp.where(pred(blk, base + row_s), s + 1.0, s),
                       jnp.zeros((slab, t), F32))
        return jnp.sum(s, axis=0, keepdims=True)

    def min_where(pred):
        s = scan_pairs(lambda s, blk, base: jnp.minimum(s, jnp.where(pred(blk), blk, inf)),
                       jnp.full((slab, t), inf, F32))
        return jnp.min(s, axis=0, keepdims=True)
    done0 = jnp.where(qpos + 1 > topk, 0.0, 1.0)
    tau0 = jnp.full((1, t), -inf, F32)
    hi0 = mx + (jnp.abs(mx) + 1.0)

    def cond_a(st):
        it, _, _, _, done = st
        return jnp.logical_and(it < 40, jnp.min(done) < 0.5)

    def body_a(st):
        it, lo, hi, tau, done = st
        mid = lo + (hi - lo) * 0.5
        c = count(lambda blk, _: blk >= mid)
        hit = jnp.logical_and(c == kf, done < 0.5)
        tau = jnp.where(hit, mid, tau)
        done = jnp.where(hit, 1.0, done)
        ge = c >= kf
        return it + 1, jnp.where(ge, mid, lo), jnp.where(ge, hi, mid), tau, done

    n_blind = jnp.where(qb * t + t > topk, BISECT_BLIND_STEPS, 0)
    st = lax.fori_loop(0, n_blind, lambda _, s: body_a(s), (jnp.int32(0), mn, hi0, tau0, done0))
    _, lo, hi, tau, done = lax.while_loop(cond_a, body_a, st)

    def cond_b(st):
        it, _, _, _, done, _, _ = st
        return jnp.logical_and(it < 4096, jnp.min(done) < 0.5)

    def body_b(st):
        it, lo, hi, tau, done, tie, need = st
        vlo = min_where(lambda blk: blk >= lo)
        ngt = count(lambda blk, _: blk > vlo)
        is_tie = jnp.logical_and(ngt < kf, done < 0.5)
        tau = jnp.where(is_tie, vlo, tau)
        need = jnp.where(is_tie, kf - ngt, need)
        tie = jnp.where(is_tie, 1.0, tie)
        done = jnp.where(is_tie, 1.0, done)
        lo2 = min_where(lambda blk: blk > vlo)
        mid = lo2 + (hi - lo2) * 0.5
        c = count(lambda blk, _: blk >= mid)
        hit = jnp.logical_and(c == kf, done < 0.5)
        tau = jnp.where(hit, mid, tau)
        done = jnp.where(hit, 1.0, done)
        ge = c >= kf
        return it + 1, jnp.where(ge, mid, lo2), jnp.where(ge, hi, mid), tau, done, tie, need

    zeros = jnp.zeros((1, t), F32)
    _, _, _, tau, done, tie, need = lax.while_loop(
        cond_b, body_b, (jnp.int32(0), lo, hi, tau, done, zeros, zeros))

    nbits = int(seq_len).bit_length() + 1
    ntrip = jnp.where(jnp.max(tie) > 0.5, nbits, 0)

    def body_j(_, st):
        jlo, jhi = st
        jm = (jlo + jhi) >> 1
        c = count(lambda blk, kpos: jnp.logical_and(blk == tau, kpos <= jm))
        ok = c >= need
        return jnp.where(ok, jlo, jm), jnp.where(ok, jm, jhi)

    _, jhi = lax.fori_loop(0, ntrip, body_j,
                           (jnp.full((1, t), -1, jnp.int32), jnp.full((1, t), seq_len - 1, jnp.int32)))
    jstar = jnp.where(tie > 0.5, jhi, -1)

    ones_rows = jnp.ones((16, t), _MXU)

    @pl.when(ntrip > 0)
    def _():
        def drop_tail(c, carry):
            blk = I_ref[c]
            drop = jnp.logical_and(jnp.logical_and(blk == tau, (c * t + rowi) > jstar), tie > 0.5)
            I_ref[c] = jnp.where(drop, -inf, blk)
            return carry
        lax.fori_loop(0, nk, drop_tail, 0)

    tau_eff = jnp.where(done0 > 0.5, -3.0e38, tau)

    def selected(c):
        return I_ref[c] >= tau_eff

    mb = (HEAD_DIM * att_scale) * (jnp.max(jnp.abs(qnw_ref[...]), axis=1, keepdims=True)
                                   * jnp.max(jnp.abs(knw_ref[...]), axis=1, keepdims=True))
    l_ref[...] = jnp.zeros(l_ref.shape, F32)
    acc_ref[...] = jnp.zeros(acc_ref.shape, F32)

    def p3_bounded(j, carry):
        for u in range(2):
            c = 2 * j + u
            bias_ref[u] = jnp.where(selected(c), -mb, NEG)
            ls = []
            for h in range(ATTN_HEADS):
                p = h // 2
                rows = slice(h * HEAD_DIM, (h + 1) * HEAD_DIM)
                s = _dot(k_ref[c, :, p * 128:(p + 1) * 128], qT_ref[h]) + bias_ref[u]
                pb = jnp.exp2(s).astype(_MXU)
                ls.append(_dot(ones_rows, pb)[0:1])
                acc_ref[rows, :] += _dot(vT_ref[c, rows, :], pb)
            l_ref[...] += jnp.concatenate(ls, axis=0)
        return carry

    lax.fori_loop(0, nk2, p3_bounded, 0)

    underflow = jnp.logical_not(jnp.min(l_ref[...]) > 2.0 ** -60)

    def p3(c, carry):
        bias_ref[0] = jnp.where(selected(c), 0.0, NEG)
        mcs = []
        for h in range(ATTN_HEADS):
            p = h // 2
            s = _dot(k_ref[c, :, p * 128:(p + 1) * 128], qT_ref[h]) + bias_ref[0]
            s_ref[h] = s
            mcs.append(jnp.max(s, axis=0, keepdims=True))
        m_old = m_ref[...]
        m_new = jnp.maximum(m_old, jnp.concatenate(mcs, axis=0))
        alpha = jnp.exp2(m_old - m_new)
        m_ref[...] = m_new
        ls = []
        for h in range(ATTN_HEADS):
            rows = slice(h * HEAD_DIM, (h + 1) * HEAD_DIM)
            pb = jnp.exp2(s_ref[h] - m_new[h:h + 1]).astype(_MXU)
            ls.append(_dot(ones_rows, pb)[0:1])
            acc_ref[rows, :] = alpha[h:h + 1] * acc_ref[rows, :] + _dot(vT_ref[c, rows, :], pb)
        l_ref[...] = alpha * l_ref[...] + jnp.concatenate(ls, axis=0)
        return carry

    @pl.when(underflow)
    def _():
        m_ref[...] = jnp.full(m_ref.shape, NEG, F32)
        l_ref[...] = jnp.zeros(l_ref.shape, F32)
        acc_ref[...] = jnp.zeros(acc_ref.shape, F32)
        lax.fori_loop(0, nk, p3, 0)

    for h in range(ATTN_HEADS):
        acc_ref[h * HEAD_DIM:(h + 1) * HEAD_DIM, :] = acc_ref[h * HEAD_DIM:(h + 1) * HEAD_DIM, :] / l_ref[h:h + 1, :]
    yt = acc_ref[...]
    ms = jnp.mean(yt * yt, axis=0, keepdims=True)
    o_ref[...] = (yt * lax.rsqrt(ms + EPS) * anw_ref[...]).T.astype(o_ref.dtype)


def _dsa(qT, iqT, iwT, k4, vT4, ik4, anw_col, qnw, knw, *, B, L, topk, att_scale):
    t = TILE
    nl = L // t
    n = B * L
    return pl.pallas_call(
        functools.partial(_dsa_kernel, topk=topk, seq_len=L, att_scale=att_scale),
        grid=(B, nl),
        in_specs=[pl.BlockSpec((None, ATTN_HEADS, 128, t), lambda b, q: (b, 0, 0, q)),
                  pl.BlockSpec((None, IDX_HEADS, 128, t), lambda b, q: (b, 0, 0, q)),
                  pl.BlockSpec((None, IDX_HEADS, t), lambda b, q: (b, 0, q)),
                  pl.BlockSpec((None, nl, t, 512), lambda b, q: (b, 0, 0, 0), pipeline_mode=pl.Buffered(1)),
                  pl.BlockSpec((None, nl, 512, t), lambda b, q: (b, 0, 0, 0), pipeline_mode=pl.Buffered(1)),
                  pl.BlockSpec((None, nl, t, 128), lambda b, q: (b, 0, 0, 0), pipeline_mode=pl.Buffered(1)),
                  pl.BlockSpec((512, 1), lambda b, q: (0, 0)),
                  pl.BlockSpec((1, 512), lambda b, q: (0, 0)),
                  pl.BlockSpec((1, 512), lambda b, q: (0, 0))],
        out_specs=pl.BlockSpec((t, 512), lambda b, q: (b * nl + q, 0)),
        out_shape=jax.ShapeDtypeStruct((n, 512), _MXU),
        scratch_shapes=[pltpu.VMEM((nl, t, t), F32),
                        pltpu.VMEM((ATTN_HEADS, t), F32),
                        pltpu.VMEM((ATTN_HEADS, t), F32),
                        pltpu.VMEM((512, t), F32),
                        pltpu.VMEM((2, t, t), F32),
                        pltpu.VMEM((ATTN_HEADS, t, t), F32)],
        compiler_params=_cparams(("arbitrary", "arbitrary")),
        name="dsa",
    )(qT, iqT, iwT, k4, vT4, ik4, anw_col, qnw, knw)


def _mix_out_kernel(x_ref, ya_ref, yc_ref, mod_ref, woa_ref, woc_ref, n2_ref, wqT_ref, sk_ref,
                    x1_ref, h2T_ref, sT_ref):
    proj = _dot(ya_ref[...], woa_ref[...]) + _dot(yc_ref[...], woc_ref[...])
    x1 = x_ref[...] + mod_ref[2:3, :] * proj
    x1_ref[...] = x1
    ms = jnp.mean(x1 * x1, axis=-1, keepdims=True)
    h2 = x1 * lax.rsqrt(ms + EPS) * n2_ref[...] * (1.0 + mod_ref[4:5, :]) + mod_ref[3:4, :]
    h2t = h2.T.astype(_MXU)
    h2T_ref[...] = h2t
    qpt = _dot(wqT_ref[...], h2t).astype(_MXU)
    for hp in range(2 * PEER_HEADS):
        sT_ref[hp] = _dot(sk_ref[hp], qpt[hp * 128:(hp + 1) * 128])


def _mix_out(x2, ya, yc, mod3, woa, woc, norm2_w, wqT, sk, *, B, L):
    n, d = x2.shape
    t = TILE
    nl = L // t
    full = lambda shape: pl.BlockSpec(shape, lambda i: (0,) * len(shape))
    return pl.pallas_call(
        _mix_out_kernel,
        grid=(n // t,),
        in_specs=[pl.BlockSpec((t, d), lambda i: (i, 0)),
                  pl.BlockSpec((t, 512), lambda i: (i, 0)),
                  pl.BlockSpec((t, 512), lambda i: (i, 0)),
                  pl.BlockSpec((None, 6, d), lambda i: (i // nl, 0, 0)),
                  full(woa.shape), full(woc.shape), full((1, d)), full(wqT.shape), full(sk.shape)],
        out_specs=(pl.BlockSpec((t, d), lambda i: (i, 0)),
                   pl.BlockSpec((d, t), lambda i: (0, i)),
                   pl.BlockSpec((2 * PEER_HEADS, PEER_KEYS, t), lambda i: (0, 0, i))),
        out_shape=(jax.ShapeDtypeStruct((n, d), F32),
                   jax.ShapeDtypeStruct((d, n), _MXU),
                   jax.ShapeDtypeStruct((2 * PEER_HEADS, PEER_KEYS, n), F32)),
        compiler_params=_cparams(("arbitrary",)),
        name="mix_out",
    )(x2, ya, yc, mod3, woa, woc, norm2_w, wqT, sk)


def _extract_topk(vals, n_take, tie_break):
    r, t = vals.shape
    rowi = lax.broadcasted_iota(jnp.int32, (r, t), 0) if tie_break else None
    work = vals
    rank = jnp.full((r, t), 99.0, F32)
    taken = []
    for kk in range(n_take):
        m = jnp.max(work, axis=0, keepdims=True)
        sel = work == m
        if tie_break:
            sel = rowi == jnp.min(jnp.where(sel, rowi, r), axis=0, keepdims=True)
        rank = jnp.where(sel, float(kk), rank)
        work = jnp.where(sel, -float("inf"), work)
        taken.append(m)
    return taken, rank


_COMBO_B_LIMIT = {1: 8, 2: 5, 3: 4, 4: 3, 5: 2, 6: 2, 7: 2}


def _route_head(s1, s2, tie_break):
    kk = PEER_TOPK
    ninf = -float("inf")
    v1, rank1 = _extract_topk(s1, kk, tie_break)
    v2, rank2 = _extract_topk(s2, kk, tie_break)
    v1all = jnp.concatenate(v1, axis=0)
    v2all = jnp.concatenate(v2, axis=0)
    sub = lax.broadcasted_iota(jnp.int32, (8, s1.shape[1]), 0)
    pieces = [v1[0] + v2all]
    for a in range(1, 8):
        piece = v1[a] + v2all[0:8]
        if _COMBO_B_LIMIT[a] < 8:
            piece = jnp.where(sub < _COMBO_B_LIMIT[a], piece, ninf)
        pieces.append(piece)
    pieces.append(v1all[8:16] + v2[0])
    combo = jnp.concatenate(pieces, axis=0)
    _, crank = _extract_topk(combo, kk, True)
    selc = jnp.where(crank < float(kk), 1.0, 0.0)
    z = jnp.sum(selc * jnp.exp(combo - combo[0:1]), axis=0, keepdims=True)
    cnts = [jnp.sum(selc[0:16], axis=0, keepdims=True)]
    cnts += [jnp.sum(selc[8 + 8 * a:16 + 8 * a], axis=0, keepdims=True) for a in range(1, 8)]
    cnts += [selc[72 + a:73 + a] for a in range(8)]
    cnt1 = jnp.zeros_like(s1)
    for a in range(kk):
        cnt1 = cnt1 + jnp.where(rank1 == float(a), cnts[a], 0.0)
    f2 = jnp.where(rank2 < float(kk), jnp.exp(s2 - v2[0]), 0.0)
    e1 = jnp.where(rank1 < float(kk), jnp.exp(s1 - v1[0]), 0.0) / z
    n_ranked = (jnp.sum(jnp.where(rank1 < float(kk), 1.0, 0.0), axis=0, keepdims=True)
                + jnp.sum(jnp.where(rank2 < float(kk), 1.0, 0.0), axis=0, keepdims=True))
    return rank2, f2, cnt1, e1, n_ranked


def _route_kernel(sT_ref, r2_ref, f2_ref, c1_ref, e1_ref):
    def run(tie_break):
        worst = jnp.zeros((1, sT_ref.shape[2]), F32)
        for h in range(PEER_HEADS):
            rank2, f2, cnt1, e1, n_ranked = _route_head(sT_ref[2 * h], sT_ref[2 * h + 1], tie_break)
            r2_ref[h] = rank2.astype(r2_ref.dtype)
            f2_ref[h] = f2.astype(f2_ref.dtype)
            c1_ref[h] = cnt1
            e1_ref[h] = e1
            worst = jnp.maximum(worst, n_ranked)
        return jnp.max(worst)

    most_ranked = run(False)

    @pl.when(most_ranked > 2.0 * PEER_TOPK)
    def _():
        run(True)


def _route(sT):
    hp, nkeys, n = sT.shape
    t = 128
    spec = pl.BlockSpec((PEER_HEADS, nkeys, t), lambda i: (0, 0, i))
    shp = lambda dt: jax.ShapeDtypeStruct((PEER_HEADS, nkeys, n), dt)
    return pl.pallas_call(
        _route_kernel,
        grid=(n // t,),
        in_specs=[pl.BlockSpec((hp, nkeys, t), lambda i: (0, 0, i))],
        out_specs=(spec, spec, spec, spec),
        out_shape=(shp(_MXU), shp(_MXU), shp(F32), shp(F32)),
        compiler_params=_cparams(("arbitrary",)),
        name="route",
    )(sT)


PEER_TM = 512
PEER_TE = 2048


def _gelu(a):
    c0 = 0.7978845608028654
    inner = a * (c0 + (c0 * 0.044715) * (a * a))
    return (0.5 * a) * (1.0 + jnp.tanh(inner))


def _peer_kernel(h2T_ref, r2_ref, f2_ref, c1_ref, e1_ref, u_ref, vT_ref, x1_ref, mod_ref, o_ref,
                 acc_ref, g_ref):
    e = pl.program_id(1)
    sub = PEER_TE // PEER_KEYS

    @pl.when(e == 0)
    def _():
        acc_ref[...] = jnp.zeros_like(acc_ref)

    for ii in range(sub):
        rows = slice(ii * PEER_KEYS, (ii + 1) * PEER_KEYS)
        a = _dot(u_ref[rows, :], h2T_ref[...]).astype(_MXU)
        i = e * sub + ii
        w = jnp.zeros((PEER_KEYS, PEER_TM), _MXU)
        for h in range(PEER_HEADS):
            c1 = c1_ref[h, pl.ds(i, 1), :].astype(_MXU)
            e1 = e1_ref[h, pl.ds(i, 1), :].astype(_MXU)
            w = w + jnp.where(r2_ref[h] < c1, f2_ref[h], jnp.zeros((), _MXU)) * e1
        g_ref[rows, :] = _gelu(a) * w
    acc_ref[...] += _dot(vT_ref[...], g_ref[...])

    @pl.when(e == pl.num_programs(1) - 1)
    def _():
        o_ref[...] = x1_ref[...] + mod_ref[5:6, :] * acc_ref[...].T


def _peer(h2T, r2, f2, c1, e1, u_b, vT_b, x1, mod3, *, B, L):
    d, n = h2T.shape
    ne = u_b.shape[0]
    tm, te = PEER_TM, PEER_TE
    ntl = L // tm
    rspec = pl.BlockSpec((PEER_HEADS, PEER_KEYS, tm), lambda i, e: (0, 0, i))
    return pl.pallas_call(
        _peer_kernel,
        grid=(n // tm, ne // te),
        in_specs=[pl.BlockSpec((d, tm), lambda i, e: (0, i)),
                  rspec, rspec, rspec, rspec,
                  pl.BlockSpec((te, d), lambda i, e: (e, 0)),
                  pl.BlockSpec((d, te), lambda i, e: (0, e)),
                  pl.BlockSpec((tm, d), lambda i, e: (i, 0)),
                  pl.BlockSpec((None, 6, d), lambda i, e: (i // ntl, 0, 0))],
        out_specs=pl.BlockSpec((tm, d), lambda i, e: (i, 0)),
        out_shape=jax.ShapeDtypeStruct((n, d), F32),
        scratch_shapes=[pltpu.VMEM((d, tm), F32), pltpu.VMEM((te, tm), _MXU)],
        compiler_params=_cparams(("arbitrary", "arbitrary")),
        name="peer",
    )(h2T, r2, f2, c1, e1, u_b, vT_b, x1, mod3)


def kernel(x, c, norm1_w, norm2_w, w_ada, b_ada, w_in, q_norm_w, k_norm_w, conv_w, conv_b,
           attn_out_norm_w, conv_out_norm_w, w_out, peer_wq, peer_subkeys, peer_u, peer_v):
    B, L, D = x.shape
    n = B * L
    assert D == 1024 and L % PEER_TM == 0 and L % (2 * TILE) == 0
    topk = min(MAX_TOPK, L // 4)
    att_scale = HEAD_DIM ** -0.5 * LOG2E
    idx_scale = (IDX_DIM ** -0.5) * (IDX_HEADS ** -0.5)

    c_pad = jnp.pad(c, ((0, 8 - B % 8 if B % 8 else 0), (0, 0)))
    w_ik = w_in[:, 3584:3648]
    w_iw = jnp.pad(w_in[:, 3648:3656], ((0, 0), (0, 120)))
    w_cat = jnp.concatenate([w_in[:, :3584], w_ik, w_ik, w_iw], axis=1).astype(_MXU)
    tile8 = lambda w: jnp.tile(w, ATTN_HEADS).reshape(1, 512)
    head_id = jnp.arange(512) // HEAD_DIM
    ones_bd = (head_id[:, None] == head_id[None, :]).astype(_MXU)
    woa = w_out[:512].astype(_MXU)
    woc = w_out[512:].astype(_MXU)
    wqT = peer_wq.T.astype(_MXU)
    sk = peer_subkeys.reshape(2 * PEER_HEADS, PEER_KEYS, PEER_KEYS).astype(_MXU)
    u_b = peer_u.astype(_MXU)
    vT_b = peer_v.T.astype(_MXU)

    mod = _ada(c_pad, w_ada, b_ada.reshape(1, -1))[:B]
    mod3 = mod.reshape(B, 6, D)
    x2 = x.reshape(n, D)

    qT, k, vT4, iqT, ik, iwT, yc = _mix_in(
        x2, mod3, norm1_w.reshape(1, D), w_cat, tile8(q_norm_w), tile8(k_norm_w),
        conv_w, conv_b.reshape(1, 512), conv_out_norm_w.reshape(1, 512), ones_bd,
        B=B, L=L, att_scale=att_scale, idx_scale=idx_scale)
    nl = L // TILE
    ya = _dsa(qT, iqT, iwT, k.reshape(B, nl, TILE, 512), vT4, ik.reshape(B, nl, TILE, 128),
              attn_out_norm_w.reshape(512, 1), tile8(q_norm_w), tile8(k_norm_w),
              B=B, L=L, topk=topk, att_scale=att_scale)
    x1, h2T, sT = _mix_out(x2, ya, yc, mod3, woa, woc, norm2_w.reshape(1, D), wqT, sk, B=B, L=L)
    r2, f2, c1, e1 = _route(sT)
    out = _peer(h2T, r2, f2, c1, e1, u_b, vT_b, x1, mod3, B=B, L=L)
    return out.reshape(B, L, D)
```

```python
import functools

import jax
import jax.numpy as jnp
from jax import lax
from jax.experimental import pallas as pl
from jax.experimental.pallas import tpu as pltpu

F32 = jnp.float32
_MXU = jnp.bfloat16

EPS = 1e-6
ATTN_HEADS = 8
HEAD_DIM = 64
IDX_HEADS = 8
IDX_DIM = 64
MAX_TOPK = 256
PEER_HEADS = 8
PEER_KEYS = 128
PEER_TOPK = 16
NEG = -1e30
LOG2E = 1.4426950408889634

TILE = 256
DSA_TQ = 512
BISECT_BLIND_STEPS = 19
VMEM_LIMIT = 56 * 1024 * 1024


def _dot(a, b):
    return jnp.dot(a, b, preferred_element_type=F32)


def _cparams(sem, flags=None):
    return pltpu.CompilerParams(dimension_semantics=sem, vmem_limit_bytes=VMEM_LIMIT, flags=flags)


def _ada_kernel(c_ref, w_ref, b_ref, o_ref):
    c = c_ref[...]
    s = c * jax.nn.sigmoid(c)
    o_ref[...] = _dot(s.astype(_MXU), w_ref[...].astype(_MXU)) + b_ref[...]


def _ada(c_pad, w_ada, b_ada):
    rows, d = c_pad.shape
    n = w_ada.shape[1]
    tn = 1024
    return pl.pallas_call(
        _ada_kernel,
        grid=(n // tn,),
        in_specs=[pl.BlockSpec((rows, d), lambda j: (0, 0)),
                  pl.BlockSpec((d, tn), lambda j: (0, j)),
                  pl.BlockSpec((1, tn), lambda j: (0, j))],
        out_specs=pl.BlockSpec((rows, tn), lambda j: (0, j)),
        out_shape=jax.ShapeDtypeStruct((rows, n), F32),
        compiler_params=_cparams(("arbitrary",)),
        name="ada",
    )(c_pad, w_ada, b_ada)


def _split_pairs(zt, out_ref):
    t = zt.shape[1]
    row = lax.broadcasted_iota(jnp.int32, (128, t), 0)
    for p in range(4):
        pair = zt[p * 128:(p + 1) * 128]
        out_ref[2 * p] = jnp.where(row < 64, pair, 0.0).astype(out_ref.dtype)
        out_ref[2 * p + 1] = jnp.where(row >= 64, pair, 0.0).astype(out_ref.dtype)


def _mix_in_kernel(x_ref, mod_ref, n1_ref, w_ref, qnw_ref, knw_ref, cw_ref, cb_ref, cnw_ref, ones_ref,
                   qT_ref, k_ref, vT_ref, iqT_ref, ik_ref, iwT_ref, yc_ref, prev_ref,
                   *, att_scale, idx_scale):
    li = pl.program_id(1)
    x = x_ref[...]
    t = x.shape[0]
    ms = jnp.mean(x * x, axis=-1, keepdims=True)
    y = x * lax.rsqrt(ms + EPS) * n1_ref[...]
    h = (y * (1.0 + mod_ref[1:2, :]) + mod_ref[0:1, :]).astype(_MXU)

    def proj(c0, c1):
        return _dot(h, w_ref[:, c0:c1])

    ones = ones_ref[...]

    def head_norm(z, w):
        z2 = z * z
        hi = z2.astype(_MXU)
        lo = (z2 - hi.astype(F32)).astype(_MXU)
        ss = _dot(hi, ones) + _dot(lo, ones)
        return z * lax.rsqrt(ss * (1.0 / HEAD_DIM) + EPS) * w

    q = head_norm(proj(0, 512), qnw_ref[...]) * att_scale
    _split_pairs(q.T, qT_ref)
    k = head_norm(proj(512, 1024), knw_ref[...])
    k_ref[...] = k.astype(k_ref.dtype)
    vT_ref[...] = proj(1024, 1536).T.astype(vT_ref.dtype)
    _split_pairs(proj(3072, 3584).T, iqT_ref)
    tail = proj(3584, 3840)
    ik_ref[...] = tail[:, 0:128].astype(ik_ref.dtype)
    iwT_ref[...] = tail[:, 128:256].T[0:IDX_HEADS] * idx_scale

    cgate = proj(1536, 2048)
    u = proj(2048, 2560) * proj(2560, 3072)

    @pl.when(li == 0)
    def _():
        prev_ref[...] = jnp.zeros_like(prev_ref)

    prev = prev_ref[...]
    row = lax.broadcasted_iota(jnp.int32, (t, 1), 0)
    u1 = jnp.where(row == 0, prev[7:8], pltpu.roll(u, 1, 0))
    u2 = jnp.where(row == 0, prev[6:7], jnp.where(row == 1, prev[7:8], pltpu.roll(u, 2, 0)))
    prev_ref[...] = u[t - 8:t]
    yc = cgate * (cb_ref[...] + cw_ref[0:1] * u2 + cw_ref[1:2] * u1 + cw_ref[2:3] * u)
    msc = jnp.mean(yc * yc, axis=-1, keepdims=True)
    yc_ref[...] = (yc * lax.rsqrt(msc + EPS) * cnw_ref[...]).astype(yc_ref.dtype)


def _mix_in(x2, mod3, norm1_w, w_cat, qnw, knw, conv_w, conv_b, cnw, ones_bd, *, B, L, att_scale, idx_scale):
    n, d = x2.shape
    t = TILE
    nl = L // t
    full = lambda shape: pl.BlockSpec(shape, lambda b, l: (0,) * len(shape))
    tok = lambda w: pl.BlockSpec((t, w), lambda b, l: (b * nl + l, 0))
    out_shapes = (
        jax.ShapeDtypeStruct((B, ATTN_HEADS, 128, L), _MXU),
        jax.ShapeDtypeStruct((n, 512), _MXU),
        jax.ShapeDtypeStruct((B, nl, 512, t), _MXU),
        jax.ShapeDtypeStruct((B, IDX_HEADS, 128, L), _MXU),
        jax.ShapeDtypeStruct((n, 128), _MXU),
        jax.ShapeDtypeStruct((B, IDX_HEADS, L), F32),
        jax.ShapeDtypeStruct((n, 512), _MXU),
    )
    out_specs = (
        pl.BlockSpec((None, ATTN_HEADS, 128, t), lambda b, l: (b, 0, 0, l)),
        tok(512),
        pl.BlockSpec((None, None, 512, t), lambda b, l: (b, l, 0, 0)),
        pl.BlockSpec((None, IDX_HEADS, 128, t), lambda b, l: (b, 0, 0, l)),
        tok(128),
        pl.BlockSpec((None, IDX_HEADS, t), lambda b, l: (b, 0, l)),
        tok(512),
    )
    return pl.pallas_call(
        functools.partial(_mix_in_kernel, att_scale=att_scale, idx_scale=idx_scale),
        grid=(B, nl),
        in_specs=[tok(d),
                  pl.BlockSpec((None, 6, d), lambda b, l: (b, 0, 0)),
                  full((1, d)), full(w_cat.shape), full((1, 512)), full((1, 512)),
                  full((3, 512)), full((1, 512)), full((1, 512)), full((512, 512))],
        out_specs=out_specs,
        out_shape=out_shapes,
        scratch_shapes=[pltpu.VMEM((8, 512), F32)],
        compiler_params=_cparams(("arbitrary", "arbitrary")),
        name="mix_in",
    )(x2, mod3, norm1_w, w_cat, qnw, knw, conv_w, conv_b, cnw, ones_bd)


def _dsa_kernel(qT_ref, iqT_ref, iwT_ref, k_ref, vT_ref, ik_ref, anw_ref, qnw_ref, knw_ref, o_ref,
                I_ref, m_ref, l_ref, acc_ref, bias_ref, s_ref, *, topk, seq_len, att_scale):
    t = DSA_TQ
    kc = TILE
    ndiag = t // kc
    qb = pl.program_id(1)
    nk = (qb + 1) * ndiag
    kf = float(topk)
    inf = float("inf")
    qpos = qb * t + lax.broadcasted_iota(jnp.int32, (1, t), 1)
    rowi = lax.broadcasted_iota(jnp.int32, (kc, 1), 0)

    def scores(c):
        ikc = ik_ref[c]
        acc = jnp.zeros((kc, t), F32)
        for h in range(IDX_HEADS):
            r = _dot(ikc, iqT_ref[h])
            acc = acc + iwT_ref[h:h + 1, :] * jnp.maximum(r, 0.0)
        return acc

    def p1(c, carry):
        mn, mx = carry
        acc = scores(c)
        I_ref[c] = acc
        return (jnp.minimum(mn, jnp.min(acc, axis=0, keepdims=True)),
                jnp.maximum(mx, jnp.max(acc, axis=0, keepdims=True)))

    mn, mx = lax.fori_loop(0, nk - ndiag, p1, (jnp.full((1, t), inf, F32), jnp.full((1, t), -inf, F32)))
    for dchunk in range(ndiag):
        c = nk - ndiag + dchunk
        acc = scores(c)
        causal = (c * kc + rowi) <= qpos
        I_ref[c] = jnp.where(causal, acc, -inf)
        mn = jnp.minimum(mn, jnp.min(jnp.where(causal, acc, inf), axis=0, keepdims=True))
        mx = jnp.maximum(mx, jnp.max(jnp.where(causal, acc, -inf), axis=0, keepdims=True))

    @pl.when(nk % 2 == 1)
    def _():
        I_ref[nk] = jnp.full((kc, t), -inf, F32)

    nk2 = (nk + 1) // 2

    slab = 32
    per_chunk = kc // slab

    def scan_pairs(fn, init):
        def body(j, s):
            for m in range(2 * per_chunk):
                u, r = divmod(m, per_chunk)
                blk = I_ref[2 * j + u, r * slab:(r + 1) * slab, :]
                s = fn(s, blk, (2 * j + u) * kc + r * slab)
            return s
        return lax.fori_loop(0, nk2, body, init)

    row_s = lax.broadcasted_iota(jnp.int32, (slab, 1), 0)

    def count(pred):
        s = scan_pairs(lambda s, blk, base: jnp.where(pred(blk, base + row_s), s + 1.0, s),
                       jnp.zeros((slab, t), F32))
        return jnp.sum(s, axis=0, keepdims=True)

    def min_where(pred):
        s = scan_pairs(lambda s, blk, base: jnp.minimum(s, jnp.where(pred(blk), blk, inf)),
                       jnp.full((slab, t), inf, F32))
        return jnp.min(s, axis=0, keepdims=True)
    done0 = jnp.where(qpos + 1 > topk, 0.0, 1.0)
    tau0 = jnp.full((1, t), -inf, F32)
    hi0 = mx + (jnp.abs(mx) + 1.0)

    def cond_a(st):
        it, _, _, _, done = st
        return jnp.logical_and(it < 40, jnp.min(done) < 0.5)

    def body_a(st):
        it, lo, hi, tau, done = st
        mid = lo + (hi - lo) * 0.5
        c = count(lambda blk, _: blk >= mid)
        hit = jnp.logical_and(c == kf, done < 0.5)
        tau = jnp.where(hit, mid, tau)
        done = jnp.where(hit, 1.0, done)
        ge = c >= kf
        return it + 1, jnp.where(ge, mid, lo), jnp.where(ge, hi, mid), tau, done

    n_blind = jnp.where(qb * t + t > topk, BISECT_BLIND_STEPS, 0)
    st = lax.fori_loop(0, n_blind, lambda _, s: body_a(s), (jnp.int32(0), mn, hi0, tau0, done0))
    _, lo, hi, tau, done = lax.while_loop(cond_a, body_a, st)

    def cond_b(st):
        it, _, _, _, done, _, _ = st
        return jnp.logical_and(it < 4096, jnp.min(done) < 0.5)

    def body_b(st):
        it, lo, hi, tau, done, tie, need = st
        vlo = min_where(lambda blk: blk >= lo)
        ngt = count(lambda blk, _: blk > vlo)
        is_tie = jnp.logical_and(ngt < kf, done < 0.5)
        tau = jnp.where(is_tie, vlo, tau)
        need = jnp.where(is_tie, kf - ngt, need)
        tie = jnp.where(is_tie, 1.0, tie)
        done = jnp.where(is_tie, 1.0, done)
        lo2 = min_where(lambda blk: blk > vlo)
        mid = lo2 + (hi - lo2) * 0.5
        c = count(lambda blk, _: blk >= mid)
        hit = jnp.logical_and(c == kf, done < 0.5)
        tau = jnp.where(hit, mid, tau)
        done = jnp.where(hit, 1.0, done)
        ge = c >= kf
        return it + 1, jnp.where(ge, mid, lo2), jnp.where(ge, hi, mid), tau, done, tie, need

    zeros = jnp.zeros((1, t), F32)
    _, _, _, tau, done, tie, need = lax.while_loop(
        cond_b, body_b, (jnp.int32(0), lo, hi, tau, done, zeros, zeros))

    nbits = int(seq_len).bit_length() + 1
    ntrip = jnp.where(jnp.max(tie) > 0.5, nbits, 0)

    def body_j(_, st):
        jlo, jhi = st
        jm = (jlo + jhi) >> 1
        c = count(lambda blk, kpos: jnp.logical_and(blk == tau, kpos <= jm))
        ok = c >= need
        return jnp.where(ok, jlo, jm), jnp.where(ok, jm, jhi)

    _, jhi = lax.fori_loop(0, ntrip, body_j,
                           (jnp.full((1, t), -1, jnp.int32), jnp.full((1, t), seq_len - 1, jnp.int32)))
    jstar = jnp.where(tie > 0.5, jhi, -1)

    ones_rows = jnp.ones((16, kc), _MXU)

    @pl.when(ntrip > 0)
    def _():
        def drop_tail(c, carry):
            blk = I_ref[c]
            drop = jnp.logical_and(jnp.logical_and(blk == tau, (c * kc + rowi) > jstar), tie > 0.5)
            I_ref[c] = jnp.where(drop, -inf, blk)
            return carry
        lax.fori_loop(0, nk, drop_tail, 0)

    tau_eff = jnp.where(done0 > 0.5, -3.0e38, tau)

    def selected(c):
        return I_ref[c] >= tau_eff

    mb = (HEAD_DIM * att_scale) * (jnp.max(jnp.abs(qnw_ref[...]), axis=1, keepdims=True)
                                   * jnp.max(jnp.abs(knw_ref[...]), axis=1, keepdims=True))
    l_ref[...] = jnp.zeros(l_ref.shape, F32)
    acc_ref[...] = jnp.zeros(acc_ref.shape, F32)

    lane_w = 256

    def p3_bounded(j, carry):
        for u in range(2):
            c = 2 * j + u
            bias_ref[u] = jnp.where(selected(c), -mb, NEG)
            for half in range(t // lane_w):
                lanes = slice(half * lane_w, (half + 1) * lane_w)
                ls = []
                for h in range(ATTN_HEADS):
                    p = h // 2
                    rows = slice(h * HEAD_DIM, (h + 1) * HEAD_DIM)
                    s = _dot(k_ref[c, :, p * 128:(p + 1) * 128], qT_ref[h, :, lanes]) + bias_ref[u, :, lanes]
                    pb = jnp.exp2(s).astype(_MXU)
                    ls.append(_dot(ones_rows, pb)[0:1])
                    acc_ref[rows, lanes] += _dot(vT_ref[c, rows, :], pb)
                l_ref[:, lanes] += jnp.concatenate(ls, axis=0)
        return carry

    lax.fori_loop(0, nk2, p3_bounded, 0)

    underflow = jnp.logical_not(jnp.min(l_ref[...]) > 2.0 ** -60)

    def p3(c, carry):
        bias_ref[0] = jnp.where(selected(c), 0.0, NEG)
        mcs = []
        for h in range(ATTN_HEADS):
            p = h // 2
            s = _dot(k_ref[c, :, p * 128:(p + 1) * 128], qT_ref[h]) + bias_ref[0]
            s_ref[h] = s
            mcs.append(jnp.max(s, axis=0, keepdims=True))
        m_old = m_ref[...]
        m_new = jnp.maximum(m_old, jnp.concatenate(mcs, axis=0))
        alpha = jnp.exp2(m_old - m_new)
        m_ref[...] = m_new
        ls = []
        for h in range(ATTN_HEADS):
            rows = slice(h * HEAD_DIM, (h + 1) * HEAD_DIM)
            pb = jnp.exp2(s_ref[h] - m_new[h:h + 1]).astype(_MXU)
            ls.append(_dot(ones_rows, pb)[0:1])
            acc_ref[rows, :] = alpha[h:h + 1] * acc_ref[rows, :] + _dot(vT_ref[c, rows, :], pb)
        l_ref[...] = alpha * l_ref[...] + jnp.concatenate(ls, axis=0)
        return carry

    @pl.when(underflow)
    def _():
        m_ref[...] = jnp.full(m_ref.shape, NEG, F32)
        l_ref[...] = jnp.zeros(l_ref.shape, F32)
        acc_ref[...] = jnp.zeros(acc_ref.shape, F32)
        lax.fori_loop(0, nk, p3, 0)

    for h in range(ATTN_HEADS):
        acc_ref[h * HEAD_DIM:(h + 1) * HEAD_DIM, :] = acc_ref[h * HEAD_DIM:(h + 1) * HEAD_DIM, :] / l_ref[h:h + 1, :]
    yt = acc_ref[...]
    ms = jnp.mean(yt * yt, axis=0, keepdims=True)
    o_ref[...] = (yt * lax.rsqrt(ms + EPS) * anw_ref[...]).T.astype(o_ref.dtype)


def _dsa(qT, iqT, iwT, k4, vT4, ik4, anw_col, qnw, knw, *, B, L, topk, att_scale):
    t = DSA_TQ
    kc = TILE
    nl = L // kc
    nq = L // t
    n = B * L
    return pl.pallas_call(
        functools.partial(_dsa_kernel, topk=topk, seq_len=L, att_scale=att_scale),
        grid=(B, nq),
        in_specs=[pl.BlockSpec((None, ATTN_HEADS, 128, t), lambda b, q: (b, 0, 0, q)),
                  pl.BlockSpec((None, IDX_HEADS, 128, t), lambda b, q: (b, 0, 0, q)),
                  pl.BlockSpec((None, IDX_HEADS, t), lambda b, q: (b, 0, q)),
                  pl.BlockSpec((None, nl, kc, 512), lambda b, q: (b, 0, 0, 0), pipeline_mode=pl.Buffered(1)),
                  pl.BlockSpec((None, nl, 512, kc), lambda b, q: (b, 0, 0, 0), pipeline_mode=pl.Buffered(1)),
                  pl.BlockSpec((None, nl, kc, 128), lambda b, q: (b, 0, 0, 0), pipeline_mode=pl.Buffered(1)),
                  pl.BlockSpec((512, 1), lambda b, q: (0, 0)),
                  pl.BlockSpec((1, 512), lambda b, q: (0, 0)),
                  pl.BlockSpec((1, 512), lambda b, q: (0, 0))],
        out_specs=pl.BlockSpec((t, 512), lambda b, q: (b * nq + q, 0)),
        out_shape=jax.ShapeDtypeStruct((n, 512), _MXU),
        scratch_shapes=[pltpu.VMEM((nl, kc, t), F32),
                        pltpu.VMEM((ATTN_HEADS, t), F32),
                        pltpu.VMEM((ATTN_HEADS, t), F32),
                        pltpu.VMEM((512, t), F32),
                        pltpu.VMEM((2, kc, t), F32),
                        pltpu.VMEM((ATTN_HEADS, kc, t), F32)],
        compiler_params=_cparams(("arbitrary", "arbitrary")),
        name="dsa",
    )(qT, iqT, iwT, k4, vT4, ik4, anw_col, qnw, knw)


def _mix_out_kernel(x_ref, ya_ref, yc_ref, mod_ref, woa_ref, woc_ref, n2_ref, wqT_ref, sk_ref,
                    x1_ref, h2T_ref, sT_ref):
    proj = _dot(ya_ref[...], woa_ref[...]) + _dot(yc_ref[...], woc_ref[...])
    x1 = x_ref[...] + mod_ref[2:3, :] * proj
    x1_ref[...] = x1
    ms = jnp.mean(x1 * x1, axis=-1, keepdims=True)
    h2 = x1 * lax.rsqrt(ms + EPS) * n2_ref[...] * (1.0 + mod_ref[4:5, :]) + mod_ref[3:4, :]
    h2t = h2.T.astype(_MXU)
    h2T_ref[...] = h2t
    qpt = _dot(wqT_ref[...], h2t).astype(_MXU)
    for hp in range(2 * PEER_HEADS):
        sT_ref[hp] = _dot(sk_ref[hp], qpt[hp * 128:(hp + 1) * 128])


def _mix_out(x2, ya, yc, mod3, woa, woc, norm2_w, wqT, sk, *, B, L):
    n, d = x2.shape
    t = TILE
    nl = L // t
    full = lambda shape: pl.BlockSpec(shape, lambda i: (0,) * len(shape))
    return pl.pallas_call(
        _mix_out_kernel,
        grid=(n // t,),
        in_specs=[pl.BlockSpec((t, d), lambda i: (i, 0)),
                  pl.BlockSpec((t, 512), lambda i: (i, 0)),
                  pl.BlockSpec((t, 512), lambda i: (i, 0)),
                  pl.BlockSpec((None, 6, d), lambda i: (i // nl, 0, 0)),
                  full(woa.shape), full(woc.shape), full((1, d)), full(wqT.shape), full(sk.shape)],
        out_specs=(pl.BlockSpec((t, d), lambda i: (i, 0)),
                   pl.BlockSpec((d, t), lambda i: (0, i)),
                   pl.BlockSpec((2 * PEER_HEADS, PEER_KEYS, t), lambda i: (0, 0, i))),
        out_shape=(jax.ShapeDtypeStruct((n, d), F32),
                   jax.ShapeDtypeStruct((d, n), _MXU),
                   jax.ShapeDtypeStruct((2 * PEER_HEADS, PEER_KEYS, n), F32)),
        compiler_params=_cparams(("arbitrary",)),
        name="mix_out",
    )(x2, ya, yc, mod3, woa, woc, norm2_w, wqT, sk)


def _extract_topk(vals, n_take, tie_break):
    r, t = vals.shape
    rowi = lax.broadcasted_iota(jnp.int32, (r, t), 0) if tie_break else None
    work = vals
    rank = jnp.full((r, t), 99.0, F32)
    taken = []
    for kk in range(n_take):
        m = jnp.max(work, axis=0, keepdims=True)
        sel = work == m
        if tie_break:
            sel = rowi == jnp.min(jnp.where(sel, rowi, r), axis=0, keepdims=True)
        rank = jnp.where(sel, float(kk), rank)
        work = jnp.where(sel, -float("inf"), work)
        taken.append(m)
    return taken, rank


_COMBO_B_LIMIT = {1: 8, 2: 5, 3: 4, 4: 3, 5: 2, 6: 2, 7: 2}


def _route_head(s1, s2, tie_break):
    kk = PEER_TOPK
    ninf = -float("inf")
    v1, rank1 = _extract_topk(s1, kk, tie_break)
    v2, rank2 = _extract_topk(s2, kk, tie_break)
    v1all = jnp.concatenate(v1, axis=0)
    v2all = jnp.concatenate(v2, axis=0)
    sub = lax.broadcasted_iota(jnp.int32, (8, s1.shape[1]), 0)
    pieces = [v1[0] + v2all]
    for a in range(1, 8):
        piece = v1[a] + v2all[0:8]
        if _COMBO_B_LIMIT[a] < 8:
            piece = jnp.where(sub < _COMBO_B_LIMIT[a], piece, ninf)
        pieces.append(piece)
    pieces.append(v1all[8:16] + v2[0])
    combo = jnp.concatenate(pieces, axis=0)
    _, crank = _extract_topk(combo, kk, True)
    selc = jnp.where(crank < float(kk), 1.0, 0.0)
    z = jnp.sum(selc * jnp.exp(combo - combo[0:1]), axis=0, keepdims=True)
    cnts = [jnp.sum(selc[0:16], axis=0, keepdims=True)]
    cnts += [jnp.sum(selc[8 + 8 * a:16 + 8 * a], axis=0, keepdims=True) for a in range(1, 8)]
    cnts += [selc[72 + a:73 + a] for a in range(8)]
    cnt1 = jnp.zeros_like(s1)
    for a in range(kk):
        cnt1 = cnt1 + jnp.where(rank1 == float(a), cnts[a], 0.0)
    f2 = jnp.where(rank2 < float(kk), jnp.exp(s2 - v2[0]), 0.0)
    e1 = jnp.where(rank1 < float(kk), jnp.exp(s1 - v1[0]), 0.0) / z
    n_ranked = (jnp.sum(jnp.where(rank1 < float(kk), 1.0, 0.0), axis=0, keepdims=True)
                + jnp.sum(jnp.where(rank2 < float(kk), 1.0, 0.0), axis=0, keepdims=True))
    return rank2, f2, cnt1, e1, n_ranked


def _route_kernel(sT_ref, r2_ref, f2_ref, c1_ref, e1_ref):
    def run(tie_break):
        worst = jnp.zeros((1, sT_ref.shape[2]), F32)
        for h in range(PEER_HEADS):
            rank2, f2, cnt1, e1, n_ranked = _route_head(sT_ref[2 * h], sT_ref[2 * h + 1], tie_break)
            r2_ref[h] = rank2.astype(r2_ref.dtype)
            f2_ref[h] = f2.astype(f2_ref.dtype)
            c1_ref[h] = cnt1
            e1_ref[h] = e1
            worst = jnp.maximum(worst, n_ranked)
        return jnp.max(worst)

    most_ranked = run(False)

    @pl.when(most_ranked > 2.0 * PEER_TOPK)
    def _():
        run(True)


def _route(sT):
    hp, nkeys, n = sT.shape
    t = 128
    spec = pl.BlockSpec((PEER_HEADS, nkeys, t), lambda i: (0, 0, i))
    shp = lambda dt: jax.ShapeDtypeStruct((PEER_HEADS, nkeys, n), dt)
    return pl.pallas_call(
        _route_kernel,
        grid=(n // t,),
        in_specs=[pl.BlockSpec((hp, nkeys, t), lambda i: (0, 0, i))],
        out_specs=(spec, spec, spec, spec),
        out_shape=(shp(_MXU), shp(_MXU), shp(F32), shp(F32)),
        compiler_params=_cparams(("arbitrary",)),
        name="route",
    )(sT)


PEER_TM = 512
PEER_TE = 2048


def _gelu(a):
    c0 = 0.7978845608028654
    inner = a * (c0 + (c0 * 0.044715) * (a * a))
    return (0.5 * a) * (1.0 + jnp.tanh(inner))


def _peer_kernel(h2T_ref, r2_ref, f2_ref, c1_ref, e1_ref, u_ref, vT_ref, x1_ref, mod_ref, o_ref,
                 acc_ref, g_ref):
    e = pl.program_id(1)
    sub = PEER_TE // PEER_KEYS

    @pl.when(e == 0)
    def _():
        acc_ref[...] = jnp.zeros_like(acc_ref)

    for ii in range(sub):
        rows = slice(ii * PEER_KEYS, (ii + 1) * PEER_KEYS)
        a = _dot(u_ref[rows, :], h2T_ref[...]).astype(_MXU)
        i = e * sub + ii
        w = jnp.zeros((PEER_KEYS, PEER_TM), _MXU)
        for h in range(PEER_HEADS):
            c1 = c1_ref[h, pl.ds(i, 1), :].astype(_MXU)
            e1 = e1_ref[h, pl.ds(i, 1), :].astype(_MXU)
            w = w + jnp.where(r2_ref[h] < c1, f2_ref[h], jnp.zeros((), _MXU)) * e1
        g_ref[rows, :] = _gelu(a) * w
    acc_ref[...] += _dot(vT_ref[...], g_ref[...])

    @pl.when(e == pl.num_programs(1) - 1)
    def _():
        o_ref[...] = x1_ref[...] + mod_ref[5:6, :] * acc_ref[...].T


def _peer(h2T, r2, f2, c1, e1, u_b, vT_b, x1, mod3, *, B, L):
    d, n = h2T.shape
    ne = u_b.shape[0]
    tm, te = PEER_TM, PEER_TE
    ntl = L // tm
    rspec = pl.BlockSpec((PEER_HEADS, PEER_KEYS, tm), lambda i, e: (0, 0, i))
    return pl.pallas_call(
        _peer_kernel,
        grid=(n // tm, ne // te),
        in_specs=[pl.BlockSpec((d, tm), lambda i, e: (0, i)),
                  rspec, rspec, rspec, rspec,
                  pl.BlockSpec((te, d), lambda i, e: (e, 0)),
                  pl.BlockSpec((d, te), lambda i, e: (0, e)),
                  pl.BlockSpec((tm, d), lambda i, e: (i, 0)),
                  pl.BlockSpec((None, 6, d), lambda i, e: (i // ntl, 0, 0))],
        out_specs=pl.BlockSpec((tm, d), lambda i, e: (i, 0)),
        out_shape=jax.ShapeDtypeStruct((n, d), F32),
        scratch_shapes=[pltpu.VMEM((d, tm), F32), pltpu.VMEM((te, tm), _MXU)],
        compiler_params=_cparams(("arbitrary", "arbitrary")),
        name="peer",
    )(h2T, r2, f2, c1, e1, u_b, vT_b, x1, mod3)


def kernel(x, c, norm1_w, norm2_w, w_ada, b_ada, w_in, q_norm_w, k_norm_w, conv_w, conv_b,
           attn_out_norm_w, conv_out_norm_w, w_out, peer_wq, peer_subkeys, peer_u, peer_v):
    B, L, D = x.shape
    n = B * L
    assert D == 1024 and L % PEER_TM == 0 and L % (2 * TILE) == 0
    topk = min(MAX_TOPK, L // 4)
    att_scale = HEAD_DIM ** -0.5 * LOG2E
    idx_scale = (IDX_DIM ** -0.5) * (IDX_HEADS ** -0.5)

    c_pad = jnp.pad(c, ((0, 8 - B % 8 if B % 8 else 0), (0, 0)))
    w_ik = w_in[:, 3584:3648]
    w_iw = jnp.pad(w_in[:, 3648:3656], ((0, 0), (0, 120)))
    w_cat = jnp.concatenate([w_in[:, :3584], w_ik, w_ik, w_iw], axis=1).astype(_MXU)
    tile8 = lambda w: jnp.tile(w, ATTN_HEADS).reshape(1, 512)
    head_id = jnp.arange(512) // HEAD_DIM
    ones_bd = (head_id[:, None] == head_id[None, :]).astype(_MXU)
    woa = w_out[:512].astype(_MXU)
    woc = w_out[512:].astype(_MXU)
    wqT = peer_wq.T.astype(_MXU)
    sk = peer_subkeys.reshape(2 * PEER_HEADS, PEER_KEYS, PEER_KEYS).astype(_MXU)
    u_b = peer_u.astype(_MXU)
    vT_b = peer_v.T.astype(_MXU)

    mod = _ada(c_pad, w_ada, b_ada.reshape(1, -1))[:B]
    mod3 = mod.reshape(B, 6, D)
    x2 = x.reshape(n, D)

    qT, k, vT4, iqT, ik, iwT, yc = _mix_in(
        x2, mod3, norm1_w.reshape(1, D), w_cat, tile8(q_norm_w), tile8(k_norm_w),
        conv_w, conv_b.reshape(1, 512), conv_out_norm_w.reshape(1, 512), ones_bd,
        B=B, L=L, att_scale=att_scale, idx_scale=idx_scale)
    nl = L // TILE
    ya = _dsa(qT, iqT, iwT, k.reshape(B, nl, TILE, 512), vT4, ik.reshape(B, nl, TILE, 128),
              attn_out_norm_w.reshape(512, 1), tile8(q_norm_w), tile8(k_norm_w),
              B=B, L=L, topk=topk, att_scale=att_scale)
    x1, h2T, sT = _mix_out(x2, ya, yc, mod3, woa, woc, norm2_w.reshape(1, D), wqT, sk, B=B, L=L)
    r2, f2, c1, e1 = _route(sT)
    out = _peer(h2T, r2, f2, c1, e1, u_b, vT_b, x1, mod3, B=B, L=L)
    return out.reshape(B, L, D)
```

```python
import functools

import jax
import jax.numpy as jnp
from jax import lax
from jax.experimental import pallas as pl
from jax.experimental.pallas import tpu as pltpu

F32 = jnp.float32
_MXU = jnp.bfloat16

EPS = 1e-6
ATTN_HEADS = 8
HEAD_DIM = 64
IDX_HEADS = 8
IDX_DIM = 64
MAX_TOPK = 256
PEER_HEADS = 8
PEER_KEYS = 128
PEER_TOPK = 16
NEG = -1e30
LOG2E = 1.4426950408889634

TILE = 256
DSA_TQ = 256
BISECT_BLIND_STEPS = 19
SCAN_GROUP = 4
ROUTE_TM = 256
VMEM_LIMIT = 56 * 1024 * 1024


def _dot(a, b):
    return jnp.dot(a, b, preferred_element_type=F32)


def _cparams(sem, flags=None):
    return pltpu.CompilerParams(dimension_semantics=sem, vmem_limit_bytes=VMEM_LIMIT, flags=flags)


def _ada_kernel(c_ref, w_ref, b_ref, o_ref):
    c = c_ref[...]
    s = c * jax.nn.sigmoid(c)
    o_ref[...] = _dot(s.astype(_MXU), w_ref[...].astype(_MXU)) + b_ref[...]


def _ada(c_pad, w_ada, b_ada):
    rows, d = c_pad.shape
    n = w_ada.shape[1]
    tn = 1024
    return pl.pallas_call(
        _ada_kernel,
        grid=(n // tn,),
        in_specs=[pl.BlockSpec((rows, d), lambda j: (0, 0)),
                  pl.BlockSpec((d, tn), lambda j: (0, j)),
                  pl.BlockSpec((1, tn), lambda j: (0, j))],
        out_specs=pl.BlockSpec((rows, tn), lambda j: (0, j)),
        out_shape=jax.ShapeDtypeStruct((rows, n), F32),
        compiler_params=_cparams(("arbitrary",)),
        name="ada",
    )(c_pad, w_ada, b_ada)


def _split_pairs(zt, out_ref):
    t = zt.shape[1]
    row = lax.broadcasted_iota(jnp.int32, (128, t), 0)
    for p in range(4):
        pair = zt[p * 128:(p + 1) * 128]
        out_ref[2 * p] = jnp.where(row < 64, pair, 0.0).astype(out_ref.dtype)
        out_ref[2 * p + 1] = jnp.where(row >= 64, pair, 0.0).astype(out_ref.dtype)


def _mix_in_kernel(x_ref, mod_ref, n1_ref, w_ref, qnw_ref, knw_ref, cw_ref, cb_ref, cnw_ref, ones_ref,
                   qT_ref, k_ref, vT_ref, iqT_ref, ik_ref, iwT_ref, yc_ref, prev_ref,
                   *, att_scale, idx_scale):
    li = pl.program_id(1)
    x = x_ref[...]
    t = x.shape[0]
    ms = jnp.mean(x * x, axis=-1, keepdims=True)
    y = x * lax.rsqrt(ms + EPS) * n1_ref[...]
    h = (y * (1.0 + mod_ref[1:2, :]) + mod_ref[0:1, :]).astype(_MXU)

    def proj(c0, c1):
        return _dot(h, w_ref[:, c0:c1])

    ones = ones_ref[...]

    def head_norm(z, w):
        z2 = z * z
        hi = z2.astype(_MXU)
        lo = (z2 - hi.astype(F32)).astype(_MXU)
        ss = _dot(hi, ones) + _dot(lo, ones)
        return z * lax.rsqrt(ss * (1.0 / HEAD_DIM) + EPS) * w

    q = head_norm(proj(0, 512), qnw_ref[...]) * att_scale
    _split_pairs(q.T, qT_ref)
    k = head_norm(proj(512, 1024), knw_ref[...])
    k_ref[...] = k.astype(k_ref.dtype)
    vT_ref[...] = proj(1024, 1536).T.astype(vT_ref.dtype)
    _split_pairs(proj(3072, 3584).T, iqT_ref)
    tail = proj(3584, 3840)
    ik_ref[...] = tail[:, 0:128].astype(ik_ref.dtype)
    iwT_ref[...] = tail[:, 128:256].T[0:IDX_HEADS] * idx_scale

    cgate = proj(1536, 2048)
    u = proj(2048, 2560) * proj(2560, 3072)

    @pl.when(li == 0)
    def _():
        prev_ref[...] = jnp.zeros_like(prev_ref)

    prev = prev_ref[...]
    row = lax.broadcasted_iota(jnp.int32, (t, 1), 0)
    u1 = jnp.where(row == 0, prev[7:8], pltpu.roll(u, 1, 0))
    u2 = jnp.where(row == 0, prev[6:7], jnp.where(row == 1, prev[7:8], pltpu.roll(u, 2, 0)))
    prev_ref[...] = u[t - 8:t]
    yc = cgate * (cb_ref[...] + cw_ref[0:1] * u2 + cw_ref[1:2] * u1 + cw_ref[2:3] * u)
    msc = jnp.mean(yc * yc, axis=-1, keepdims=True)
    yc_ref[...] = (yc * lax.rsqrt(msc + EPS) * cnw_ref[...]).astype(yc_ref.dtype)


def _mix_in(x2, mod3, norm1_w, w_cat, qnw, knw, conv_w, conv_b, cnw, ones_bd, *, B, L, att_scale, idx_scale):
    n, d = x2.shape
    t = TILE
    nl = L // t
    full = lambda shape: pl.BlockSpec(shape, lambda b, l: (0,) * len(shape))
    tok = lambda w: pl.BlockSpec((t, w), lambda b, l: (b * nl + l, 0))
    out_shapes = (
        jax.ShapeDtypeStruct((B, ATTN_HEADS, 128, L), _MXU),
        jax.ShapeDtypeStruct((n, 512), _MXU),
        jax.ShapeDtypeStruct((B, nl, 512, t), _MXU),
        jax.ShapeDtypeStruct((B, IDX_HEADS, 128, L), _MXU),
        jax.ShapeDtypeStruct((n, 128), _MXU),
        jax.ShapeDtypeStruct((B, IDX_HEADS, L), F32),
        jax.ShapeDtypeStruct((n, 512), _MXU),
    )
    out_specs = (
        pl.BlockSpec((None, ATTN_HEADS, 128, t), lambda b, l: (b, 0, 0, l)),
        tok(512),
        pl.BlockSpec((None, None, 512, t), lambda b, l: (b, l, 0, 0)),
        pl.BlockSpec((None, IDX_HEADS, 128, t), lambda b, l: (b, 0, 0, l)),
        tok(128),
        pl.BlockSpec((None, IDX_HEADS, t), lambda b, l: (b, 0, l)),
        tok(512),
    )
    return pl.pallas_call(
        functools.partial(_mix_in_kernel, att_scale=att_scale, idx_scale=idx_scale),
        grid=(B, nl),
        in_specs=[tok(d),
                  pl.BlockSpec((None, 6, d), lambda b, l: (b, 0, 0)),
                  full((1, d)), full(w_cat.shape), full((1, 512)), full((1, 512)),
                  full((3, 512)), full((1, 512)), full((1, 512)), full((512, 512))],
        out_specs=out_specs,
        out_shape=out_shapes,
        scratch_shapes=[pltpu.VMEM((8, 512), F32)],
        compiler_params=_cparams(("arbitrary", "arbitrary")),
        name="mix_in",
    )(x2, mod3, norm1_w, w_cat, qnw, knw, conv_w, conv_b, cnw, ones_bd)


def _dsa_kernel(qT_ref, iqT_ref, iwT_ref, k_ref, vT_ref, ik_ref, anw_ref, qnw_ref, knw_ref, o_ref,
                I_ref, m_ref, l_ref, acc_ref, bias_ref, s_ref, *, topk, seq_len, att_scale):
    t = DSA_TQ
    kc = TILE
    ndiag = t // kc
    qb = pl.program_id(1)
    nk = (qb + 1) * ndiag
    kf = float(topk)
    inf = float("inf")
    qpos = qb * t + lax.broadcasted_iota(jnp.int32, (1, t), 1)
    rowi = lax.broadcasted_iota(jnp.int32, (kc, 1), 0)

    def scores(c):
        ikc = ik_ref[c]
        acc = jnp.zeros((kc, t), F32)
        for h in range(IDX_HEADS):
            r = _dot(ikc, iqT_ref[h])
            acc = acc + iwT_ref[h:h + 1, :] * jnp.maximum(r, 0.0)
        return acc

    def p1(c, carry):
        mn, mx = carry
        acc = scores(c)
        I_ref[c] = acc
        return (jnp.minimum(mn, jnp.min(acc, axis=0, keepdims=True)),
                jnp.maximum(mx, jnp.max(acc, axis=0, keepdims=True)))

    mn, mx = lax.fori_loop(0, nk - ndiag, p1, (jnp.full((1, t), inf, F32), jnp.full((1, t), -inf, F32)))
    for dchunk in range(ndiag):
        c = nk - ndiag + dchunk
        acc = scores(c)
        causal = (c * kc + rowi) <= qpos
        I_ref[c] = jnp.where(causal, acc, -inf)
        mn = jnp.minimum(mn, jnp.min(jnp.where(causal, acc, inf), axis=0, keepdims=True))
        mx = jnp.maximum(mx, jnp.max(jnp.where(causal, acc, -inf), axis=0, keepdims=True))

    nk2 = (nk + 1) // 2
    nkg = (nk + SCAN_GROUP - 1) // SCAN_GROUP
    for d in range(SCAN_GROUP - 1):
        @pl.when(nk + d < nkg * SCAN_GROUP)
        def _(d=d):
            I_ref[nk + d] = jnp.full((kc, t), -inf, F32)

    slab = 32
    per_chunk = kc // slab

    def scan_pairs(fn, init):
        def body(j, s):
            for m in range(SCAN_GROUP * per_chunk):
                u, r = divmod(m, per_chunk)
                blk = I_ref[SCAN_GROUP * j + u, r * slab:(r + 1) * slab, :]
                s = fn(s, blk, (SCAN_GROUP * j + u) * kc + r * slab)
            return s
        return lax.fori_loop(0, nkg, body, init)

    row_s = lax.broadcasted_iota(jnp.int32, (slab, 1), 0)

    def count(pred):
        s = scan_pairs(lambda s, blk, base: jnp.where(pred(blk, base + row_s), s + 1.0, s),
                       jnp.zeros((slab, t), F32))
        return jnp.sum(s, axis=0, keepdims=True)

    def min_where(pred):
        s = scan_pairs(lambda s, blk, base: jnp.minimum(s, jnp.where(pred(blk), blk, inf)),
                       jnp.full((slab, t), inf, F32))
        return jnp.min(s, axis=0, keepdims=True)
    done0 = jnp.where(qpos + 1 > topk, 0.0, 1.0)
    tau0 = jnp.full((1, t), -inf, F32)
    hi0 = mx + (jnp.abs(mx) + 1.0)

    def cond_a(st):
        it, _, _, _, done = st
        return jnp.logical_and(it < 40, jnp.min(done) < 0.5)

    def body_a(st):
        it, lo, hi, tau, done = st
        mid = lo + (hi - lo) * 0.5
        c = count(lambda blk, _: blk >= mid)
        hit = jnp.logical_and(c == kf, done < 0.5)
        tau = jnp.where(hit, mid, tau)
        done = jnp.where(hit, 1.0, done)
        ge = c >= kf
        return it + 1, jnp.where(ge, mid, lo), jnp.where(ge, hi, mid), tau, done

    n_blind = jnp.where(qb * t + t > topk, BISECT_BLIND_STEPS, 0)
    st = lax.fori_loop(0, n_blind, lambda _, s: body_a(s), (jnp.int32(0), mn, hi0, tau0, done0))
    _, lo, hi, tau, done = lax.while_loop(cond_a, body_a, st)

    def cond_b(st):
        it, _, _, _, done, _, _ = st
        return jnp.logical_and(it < 4096, jnp.min(done) < 0.5)

    def body_b(st):
        it, lo, hi, tau, done, tie, need = st
        vlo = min_where(lambda blk: blk >= lo)
        ngt = count(lambda blk, _: blk > vlo)
        is_tie = jnp.logical_and(ngt < kf, done < 0.5)
        tau = jnp.where(is_tie, vlo, tau)
        need = jnp.where(is_tie, kf - ngt, need)
        tie = jnp.where(is_tie, 1.0, tie)
        done = jnp.where(is_tie, 1.0, done)
        lo2 = min_where(lambda blk: blk > vlo)
        mid = lo2 + (hi - lo2) * 0.5
        c = count(lambda blk, _: blk >= mid)
        hit = jnp.logical_and(c == kf, done < 0.5)
        tau = jnp.where(hit, mid, tau)
        done = jnp.where(hit, 1.0, done)
        ge = c >= kf
        return it + 1, jnp.where(ge, mid, lo2), jnp.where(ge, hi, mid), tau, done, tie, need

    zeros = jnp.zeros((1, t), F32)
    _, _, _, tau, done, tie, need = lax.while_loop(
        cond_b, body_b, (jnp.int32(0), lo, hi, tau, done, zeros, zeros))

    nbits = int(seq_len).bit_length() + 1
    ntrip = jnp.where(jnp.max(tie) > 0.5, nbits, 0)

    def body_j(_, st):
        jlo, jhi = st
        jm = (jlo + jhi) >> 1
        c = count(lambda blk, kpos: jnp.logical_and(blk == tau, kpos <= jm))
        ok = c >= need
        return jnp.where(ok, jlo, jm), jnp.where(ok, jm, jhi)

    _, jhi = lax.fori_loop(0, ntrip, body_j,
                           (jnp.full((1, t), -1, jnp.int32), jnp.full((1, t), seq_len - 1, jnp.int32)))
    jstar = jnp.where(tie > 0.5, jhi, -1)

    ones_rows = jnp.ones((16, kc), _MXU)

    @pl.when(ntrip > 0)
    def _():
        def drop_tail(c, carry):
            blk = I_ref[c]
            drop = jnp.logical_and(jnp.logical_and(blk == tau, (c * kc + rowi) > jstar), tie > 0.5)
            I_ref[c] = jnp.where(drop, -inf, blk)
            return carry
        lax.fori_loop(0, nk, drop_tail, 0)

    tau_eff = jnp.where(done0 > 0.5, -3.0e38, tau)

    def selected(c):
        return I_ref[c] >= tau_eff

    mb = (HEAD_DIM * att_scale) * (jnp.max(jnp.abs(qnw_ref[...]), axis=1, keepdims=True)
                                   * jnp.max(jnp.abs(knw_ref[...]), axis=1, keepdims=True))
    l_ref[...] = jnp.zeros(l_ref.shape, F32)
    acc_ref[...] = jnp.zeros(acc_ref.shape, F32)

    lane_w = 256

    def p3_bounded(j, carry):
        for u in range(2):
            c = 2 * j + u
            bias_ref[u] = jnp.where(selected(c), -mb, NEG)
            for half in range(t // lane_w):
                lanes = slice(half * lane_w, (half + 1) * lane_w)
                ls = []
                for h in range(ATTN_HEADS):
                    p = h // 2
                    rows = slice(h * HEAD_DIM, (h + 1) * HEAD_DIM)
                    s = _dot(k_ref[c, :, p * 128:(p + 1) * 128], qT_ref[h, :, lanes]) + bias_ref[u, :, lanes]
                    pb = jnp.exp2(s).astype(_MXU)
                    ls.append(_dot(ones_rows, pb)[0:1])
                    acc_ref[rows, lanes] += _dot(vT_ref[c, rows, :], pb)
                l_ref[:, lanes] += jnp.concatenate(ls, axis=0)
        return carry

    lax.fori_loop(0, nk2, p3_bounded, 0)

    underflow = jnp.logical_not(jnp.min(l_ref[...]) > 2.0 ** -60)

    def p3(c, carry):
        bias_ref[0] = jnp.where(selected(c), 0.0, NEG)
        mcs = []
        for h in range(ATTN_HEADS):
            p = h // 2
            s = _dot(k_ref[c, :, p * 128:(p + 1) * 128], qT_ref[h]) + bias_ref[0]
            s_ref[h] = s
            mcs.append(jnp.max(s, axis=0, keepdims=True))
        m_old = m_ref[...]
        m_new = jnp.maximum(m_old, jnp.concatenate(mcs, axis=0))
        alpha = jnp.exp2(m_old - m_new)
        m_ref[...] = m_new
        ls = []
        for h in range(ATTN_HEADS):
            rows = slice(h * HEAD_DIM, (h + 1) * HEAD_DIM)
            pb = jnp.exp2(s_ref[h] - m_new[h:h + 1]).astype(_MXU)
            ls.append(_dot(ones_rows, pb)[0:1])
            acc_ref[rows, :] = alpha[h:h + 1] * acc_ref[rows, :] + _dot(vT_ref[c, rows, :], pb)
        l_ref[...] = alpha * l_ref[...] + jnp.concatenate(ls, axis=0)
        return carry

    @pl.when(underflow)
    def _():
        m_ref[...] = jnp.full(m_ref.shape, NEG, F32)
        l_ref[...] = jnp.zeros(l_ref.shape, F32)
        acc_ref[...] = jnp.zeros(acc_ref.shape, F32)
        lax.fori_loop(0, nk, p3, 0)

    for h in range(ATTN_HEADS):
        acc_ref[h * HEAD_DIM:(h + 1) * HEAD_DIM, :] = acc_ref[h * HEAD_DIM:(h + 1) * HEAD_DIM, :] / l_ref[h:h + 1, :]
    yt = acc_ref[...]
    ms = jnp.mean(yt * yt, axis=0, keepdims=True)
    o_ref[...] = (yt * lax.rsqrt(ms + EPS) * anw_ref[...]).T.astype(o_ref.dtype)


def _dsa(qT, iqT, iwT, k4, vT4, ik4, anw_col, qnw, knw, *, B, L, topk, att_scale):
    t = DSA_TQ
    kc = TILE
    nl = L // kc
    nq = L // t
    n = B * L
    return pl.pallas_call(
        functools.partial(_dsa_kernel, topk=topk, seq_len=L, att_scale=att_scale),
        grid=(B, nq),
        in_specs=[pl.BlockSpec((None, ATTN_HEADS, 128, t), lambda b, q: (b, 0, 0, q)),
                  pl.BlockSpec((None, IDX_HEADS, 128, t), lambda b, q: (b, 0, 0, q)),
                  pl.BlockSpec((None, IDX_HEADS, t), lambda b, q: (b, 0, q)),
                  pl.BlockSpec((None, nl, kc, 512), lambda b, q: (b, 0, 0, 0), pipeline_mode=pl.Buffered(1)),
                  pl.BlockSpec((None, nl, 512, kc), lambda b, q: (b, 0, 0, 0), pipeline_mode=pl.Buffered(1)),
                  pl.BlockSpec((None, nl, kc, 128), lambda b, q: (b, 0, 0, 0), pipeline_mode=pl.Buffered(1)),
                  pl.BlockSpec((512, 1), lambda b, q: (0, 0)),
                  pl.BlockSpec((1, 512), lambda b, q: (0, 0)),
                  pl.BlockSpec((1, 512), lambda b, q: (0, 0))],
        out_specs=pl.BlockSpec((t, 512), lambda b, q: (b * nq + q, 0)),
        out_shape=jax.ShapeDtypeStruct((n, 512), _MXU),
        scratch_shapes=[pltpu.VMEM((nl, kc, t), F32),
                        pltpu.VMEM((ATTN_HEADS, t), F32),
                        pltpu.VMEM((ATTN_HEADS, t), F32),
                        pltpu.VMEM((512, t), F32),
                        pltpu.VMEM((2, kc, t), F32),
                        pltpu.VMEM((ATTN_HEADS, kc, t), F32)],
        compiler_params=_cparams(("arbitrary", "arbitrary")),
        name="dsa",
    )(qT, iqT, iwT, k4, vT4, ik4, anw_col, qnw, knw)


def _mix_out_kernel(x_ref, ya_ref, yc_ref, mod_ref, woa_ref, woc_ref, n2_ref, wqT_ref, sk_ref,
                    x1_ref, h2T_ref, sT_ref):
    proj = _dot(ya_ref[...], woa_ref[...]) + _dot(yc_ref[...], woc_ref[...])
    x1 = x_ref[...] + mod_ref[2:3, :] * proj
    x1_ref[...] = x1
    ms = jnp.mean(x1 * x1, axis=-1, keepdims=True)
    h2 = x1 * lax.rsqrt(ms + EPS) * n2_ref[...] * (1.0 + mod_ref[4:5, :]) + mod_ref[3:4, :]
    h2t = h2.T.astype(_MXU)
    h2T_ref[...] = h2t
    qpt = _dot(wqT_ref[...], h2t).astype(_MXU)
    for hp in range(2 * PEER_HEADS):
        sT_ref[hp] = _dot(sk_ref[hp], qpt[hp * 128:(hp + 1) * 128])


def _mix_out(x2, ya, yc, mod3, woa, woc, norm2_w, wqT, sk, *, B, L):
    n, d = x2.shape
    t = TILE
    nl = L // t
    full = lambda shape: pl.BlockSpec(shape, lambda i: (0,) * len(shape))
    return pl.pallas_call(
        _mix_out_kernel,
        grid=(n // t,),
        in_specs=[pl.BlockSpec((t, d), lambda i: (i, 0)),
                  pl.BlockSpec((t, 512), lambda i: (i, 0)),
                  pl.BlockSpec((t, 512), lambda i: (i, 0)),
                  pl.BlockSpec((None, 6, d), lambda i: (i // nl, 0, 0)),
                  full(woa.shape), full(woc.shape), full((1, d)), full(wqT.shape), full(sk.shape)],
        out_specs=(pl.BlockSpec((t, d), lambda i: (i, 0)),
                   pl.BlockSpec((d, t), lambda i: (0, i)),
                   pl.BlockSpec((2 * PEER_HEADS, PEER_KEYS, t), lambda i: (0, 0, i))),
        out_shape=(jax.ShapeDtypeStruct((n, d), F32),
                   jax.ShapeDtypeStruct((d, n), _MXU),
                   jax.ShapeDtypeStruct((2 * PEER_HEADS, PEER_KEYS, n), F32)),
        compiler_params=_cparams(("arbitrary",)),
        name="mix_out",
    )(x2, ya, yc, mod3, woa, woc, norm2_w, wqT, sk)


def _extract_topk(vals, n_take, tie_break):
    r, t = vals.shape
    rowi = lax.broadcasted_iota(jnp.int32, (r, t), 0) if tie_break else None
    work = vals
    rank = jnp.full((r, t), 99.0, F32)
    taken = []
    for kk in range(n_take):
        m = jnp.max(work, axis=0, keepdims=True)
        sel = work == m
        if tie_break:
            sel = rowi == jnp.min(jnp.where(sel, rowi, r), axis=0, keepdims=True)
        rank = jnp.where(sel, float(kk), rank)
        work = jnp.where(sel, -float("inf"), work)
        taken.append(m)
    return taken, rank


_COMBO_B_LIMIT = {1: 8, 2: 5, 3: 4, 4: 3, 5: 2, 6: 2, 7: 2}


def _route_head(s1, s2, tie_break):
    kk = PEER_TOPK
    ninf = -float("inf")
    v1, rank1 = _extract_topk(s1, kk, tie_break)
    v2, rank2 = _extract_topk(s2, kk, tie_break)
    v1all = jnp.concatenate(v1, axis=0)
    v2all = jnp.concatenate(v2, axis=0)
    sub = lax.broadcasted_iota(jnp.int32, (8, s1.shape[1]), 0)
    pieces = [v1[0] + v2all]
    for a in range(1, 8):
        piece = v1[a] + v2all[0:8]
        if _COMBO_B_LIMIT[a] < 8:
            piece = jnp.where(sub < _COMBO_B_LIMIT[a], piece, ninf)
        pieces.append(piece)
    pieces.append(v1all[8:16] + v2[0])
    combo = jnp.concatenate(pieces, axis=0)
    _, crank = _extract_topk(combo, kk, True)
    selc = jnp.where(crank < float(kk), 1.0, 0.0)
    z = jnp.sum(selc * jnp.exp(combo - combo[0:1]), axis=0, keepdims=True)
    cnts = [jnp.sum(selc[0:16], axis=0, keepdims=True)]
    cnts += [jnp.sum(selc[8 + 8 * a:16 + 8 * a], axis=0, keepdims=True) for a in range(1, 8)]
    cnts += [selc[72 + a:73 + a] for a in range(8)]
    cnt1 = jnp.zeros_like(s1)
    for a in range(kk):
        cnt1 = cnt1 + jnp.where(rank1 == float(a), cnts[a], 0.0)
    f2 = jnp.where(rank2 < float(kk), jnp.exp(s2 - v2[0]), 0.0)
    e1 = jnp.where(rank1 < float(kk), jnp.exp(s1 - v1[0]), 0.0) / z
    n_ranked = (jnp.sum(jnp.where(rank1 < float(kk), 1.0, 0.0), axis=0, keepdims=True)
                + jnp.sum(jnp.where(rank2 < float(kk), 1.0, 0.0), axis=0, keepdims=True))
    return rank2, f2, cnt1, e1, n_ranked


def _route_kernel(sT_ref, r2_ref, f2_ref, c1_ref, e1_ref):
    def run(tie_break):
        worst = jnp.zeros((1, sT_ref.shape[2]), F32)
        for h in range(PEER_HEADS):
            rank2, f2, cnt1, e1, n_ranked = _route_head(sT_ref[2 * h], sT_ref[2 * h + 1], tie_break)
            r2_ref[h] = rank2.astype(r2_ref.dtype)
            f2_ref[h] = f2.astype(f2_ref.dtype)
            c1_ref[h] = cnt1
            e1_ref[h] = e1
            worst = jnp.maximum(worst, n_ranked)
        return jnp.max(worst)

    most_ranked = run(False)

    @pl.when(most_ranked > 2.0 * PEER_TOPK)
    def _():
        run(True)


def _route(sT):
    hp, nkeys, n = sT.shape
    t = ROUTE_TM
    spec = pl.BlockSpec((PEER_HEADS, nkeys, t), lambda i: (0, 0, i))
    shp = lambda dt: jax.ShapeDtypeStruct((PEER_HEADS, nkeys, n), dt)
    return pl.pallas_call(
        _route_kernel,
        grid=(n // t,),
        in_specs=[pl.BlockSpec((hp, nkeys, t), lambda i: (0, 0, i))],
        out_specs=(spec, spec, spec, spec),
        out_shape=(shp(_MXU), shp(_MXU), shp(F32), shp(F32)),
        compiler_params=_cparams(("arbitrary",)),
        name="route",
    )(sT)


PEER_TM = 512
PEER_TE = 2048


def _gelu(a):
    c0 = 0.7978845608028654
    inner = a * (c0 + (c0 * 0.044715) * (a * a))
    return (0.5 * a) * (1.0 + jnp.tanh(inner))


def _peer_kernel(h2T_ref, r2_ref, f2_ref, c1_ref, e1_ref, u_ref, vT_ref, x1_ref, mod_ref, o_ref,
                 acc_ref, g_ref):
    e = pl.program_id(1)
    sub = PEER_TE // PEER_KEYS

    @pl.when(e == 0)
    def _():
        acc_ref[...] = jnp.zeros_like(acc_ref)

    for ii in range(sub):
        rows = slice(ii * PEER_KEYS, (ii + 1) * PEER_KEYS)
        a = _dot(u_ref[rows, :], h2T_ref[...]).astype(_MXU)
        i = e * sub + ii
        w = jnp.zeros((PEER_KEYS, PEER_TM), _MXU)
        for h in range(PEER_HEADS):
            c1 = c1_ref[h, pl.ds(i, 1), :].astype(_MXU)
            e1 = e1_ref[h, pl.ds(i, 1), :].astype(_MXU)
            w = w + jnp.where(r2_ref[h] < c1, f2_ref[h], jnp.zeros((), _MXU)) * e1
        g_ref[rows, :] = _gelu(a) * w
    acc_ref[...] += _dot(vT_ref[...], g_ref[...])

    @pl.when(e == pl.num_programs(1) - 1)
    def _():
        o_ref[...] = x1_ref[...] + mod_ref[5:6, :] * acc_ref[...].T


def _peer(h2T, r2, f2, c1, e1, u_b, vT_b, x1, mod3, *, B, L):
    d, n = h2T.shape
    ne = u_b.shape[0]
    tm, te = PEER_TM, PEER_TE
    ntl = L // tm
    rspec = pl.BlockSpec((PEER_HEADS, PEER_KEYS, tm), lambda i, e: (0, 0, i))
    return pl.pallas_call(
        _peer_kernel,
        grid=(n // tm, ne // te),
        in_specs=[pl.BlockSpec((d, tm), lambda i, e: (0, i)),
                  rspec, rspec, rspec, rspec,
                  pl.BlockSpec((te, d), lambda i, e: (e, 0)),
                  pl.BlockSpec((d, te), lambda i, e: (0, e)),
                  pl.BlockSpec((tm, d), lambda i, e: (i, 0)),
                  pl.BlockSpec((None, 6, d), lambda i, e: (i // ntl, 0, 0))],
        out_specs=pl.BlockSpec((tm, d), lambda i, e: (i, 0)),
        out_shape=jax.ShapeDtypeStruct((n, d), F32),
        scratch_shapes=[pltpu.VMEM((d, tm), F32), pltpu.VMEM((te, tm), _MXU)],
        compiler_params=_cparams(("arbitrary", "arbitrary")),
        name="peer",
    )(h2T, r2, f2, c1, e1, u_b, vT_b, x1, mod3)


def kernel(x, c, norm1_w, norm2_w, w_ada, b_ada, w_in, q_norm_w, k_norm_w, conv_w, conv_b,
           attn_out_norm_w, conv_out_norm_w, w_out, peer_wq, peer_subkeys, peer_u, peer_v):
    B, L, D = x.shape
    n = B * L
    assert D == 1024 and L % PEER_TM == 0 and L % (SCAN_GROUP * TILE) == 0 and L % DSA_TQ == 0
    topk = min(MAX_TOPK, L // 4)
    att_scale = HEAD_DIM ** -0.5 * LOG2E
    idx_scale = (IDX_DIM ** -0.5) * (IDX_HEADS ** -0.5)

    c_pad = jnp.pad(c, ((0, 8 - B % 8 if B % 8 else 0), (0, 0)))
    w_ik = w_in[:, 3584:3648]
    w_iw = jnp.pad(w_in[:, 3648:3656], ((0, 0), (0, 120)))
    w_cat = jnp.concatenate([w_in[:, :3584], w_ik, w_ik, w_iw], axis=1).astype(_MXU)
    tile8 = lambda w: jnp.tile(w, ATTN_HEADS).reshape(1, 512)
    head_id = jnp.arange(512) // HEAD_DIM
    ones_bd = (head_id[:, None] == head_id[None, :]).astype(_MXU)
    woa = w_out[:512].astype(_MXU)
    woc = w_out[512:].astype(_MXU)
    wqT = peer_wq.T.astype(_MXU)
    sk = peer_subkeys.reshape(2 * PEER_HEADS, PEER_KEYS, PEER_KEYS).astype(_MXU)
    u_b = peer_u.astype(_MXU)
    vT_b = peer_v.T.astype(_MXU)

    mod = _ada(c_pad, w_ada, b_ada.reshape(1, -1))[:B]
    mod3 = mod.reshape(B, 6, D)
    x2 = x.reshape(n, D)

    qT, k, vT4, iqT, ik, iwT, yc = _mix_in(
        x2, mod3, norm1_w.reshape(1, D), w_cat, tile8(q_norm_w), tile8(k_norm_w),
        conv_w, conv_b.reshape(1, 512), conv_out_norm_w.reshape(1, 512), ones_bd,
        B=B, L=L, att_scale=att_scale, idx_scale=idx_scale)
    nl = L // TILE
    ya = _dsa(qT, iqT, iwT, k.reshape(B, nl, TILE, 512), vT4, ik.reshape(B, nl, TILE, 128),
              attn_out_norm_w.reshape(512, 1), tile8(q_norm_w), tile8(k_norm_w),
              B=B, L=L, topk=topk, att_scale=att_scale)
    x1, h2T, sT = _mix_out(x2, ya, yc, mod3, woa, woc, norm2_w.reshape(1, D), wqT, sk, B=B, L=L)
    r2, f2, c1, e1 = _route(sT)
    out = _peer(h2T, r2, f2, c1, e1, u_b, vT_b, x1, mod3, B=B, L=L)
    return out.reshape(B, L, D)
```

```python
import functools

import jax
import jax.numpy as jnp
from jax import lax
from jax.experimental import pallas as pl
from jax.experimental.pallas import tpu as pltpu

F32 = jnp.float32
_MXU = jnp.bfloat16

EPS = 1e-6
ATTN_HEADS = 8
HEAD_DIM = 64
IDX_HEADS = 8
IDX_DIM = 64
MAX_TOPK = 256
PEER_HEADS = 8
PEER_KEYS = 128
PEER_TOPK = 16
NEG = -1e30
LOG2E = 1.4426950408889634

TILE = 256
DSA_TQ = 256
BISECT_BLIND_STEPS = 19
SCAN_GROUP = 2
ROUTE_TM = 256
VMEM_LIMIT = 56 * 1024 * 1024


def _dot(a, b):
    return jnp.dot(a, b, preferred_element_type=F32)


def _cparams(sem, flags=None):
    return pltpu.CompilerParams(dimension_semantics=sem, vmem_limit_bytes=VMEM_LIMIT, flags=flags)


def _ada_kernel(c_ref, w_ref, b_ref, o_ref):
    c = c_ref[...]
    s = c * jax.nn.sigmoid(c)
    o_ref[...] = _dot(s.astype(_MXU), w_ref[...].astype(_MXU)) + b_ref[...]


def _ada(c_pad, w_ada, b_ada):
    rows, d = c_pad.shape
    n = w_ada.shape[1]
    tn = 1024
    return pl.pallas_call(
        _ada_kernel,
        grid=(n // tn,),
        in_specs=[pl.BlockSpec((rows, d), lambda j: (0, 0)),
                  pl.BlockSpec((d, tn), lambda j: (0, j)),
                  pl.BlockSpec((1, tn), lambda j: (0, j))],
        out_specs=pl.BlockSpec((rows, tn), lambda j: (0, j)),
        out_shape=jax.ShapeDtypeStruct((rows, n), F32),
        compiler_params=_cparams(("arbitrary",)),
        name="ada",
    )(c_pad, w_ada, b_ada)


def _split_pairs(zt, out_ref):
    t = zt.shape[1]
    row = lax.broadcasted_iota(jnp.int32, (128, t), 0)
    for p in range(4):
        pair = zt[p * 128:(p + 1) * 128]
        out_ref[2 * p] = jnp.where(row < 64, pair, 0.0).astype(out_ref.dtype)
        out_ref[2 * p + 1] = jnp.where(row >= 64, pair, 0.0).astype(out_ref.dtype)


def _mix_in_kernel(x_ref, mod_ref, n1_ref, w_ref, qnw_ref, knw_ref, cw_ref, cb_ref, cnw_ref, ones_ref,
                   qT_ref, k_ref, vT_ref, iqT_ref, ik_ref, iwT_ref, yc_ref, prev_ref,
                   *, att_scale, idx_scale):
    li = pl.program_id(1)
    x = x_ref[...]
    t = x.shape[0]
    ms = jnp.mean(x * x, axis=-1, keepdims=True)
    y = x * lax.rsqrt(ms + EPS) * n1_ref[...]
    h = (y * (1.0 + mod_ref[1:2, :]) + mod_ref[0:1, :]).astype(_MXU)

    def proj(c0, c1):
        return _dot(h, w_ref[:, c0:c1])

    ones = ones_ref[...]

    def head_norm(z, w):
        z2 = z * z
        hi = z2.astype(_MXU)
        lo = (z2 - hi.astype(F32)).astype(_MXU)
        ss = _dot(hi, ones) + _dot(lo, ones)
        return z * lax.rsqrt(ss * (1.0 / HEAD_DIM) + EPS) * w

    q = head_norm(proj(0, 512), qnw_ref[...]) * att_scale
    _split_pairs(q.T, qT_ref)
    k = head_norm(proj(512, 1024), knw_ref[...])
    k_ref[...] = k.astype(k_ref.dtype)
    vT_ref[...] = proj(1024, 1536).T.astype(vT_ref.dtype)
    _split_pairs(proj(3072, 3584).T, iqT_ref)
    tail = proj(3584, 3840)
    ik_ref[...] = tail[:, 0:128].astype(ik_ref.dtype)
    iwT_ref[...] = tail[:, 128:256].T[0:IDX_HEADS] * idx_scale

    cgate = proj(1536, 2048)
    u = proj(2048, 2560) * proj(2560, 3072)

    @pl.when(li == 0)
    def _():
        prev_ref[...] = jnp.zeros_like(prev_ref)

    prev = prev_ref[...]
    row = lax.broadcasted_iota(jnp.int32, (t, 1), 0)
    u1 = jnp.where(row == 0, prev[7:8], pltpu.roll(u, 1, 0))
    u2 = jnp.where(row == 0, prev[6:7], jnp.where(row == 1, prev[7:8], pltpu.roll(u, 2, 0)))
    prev_ref[...] = u[t - 8:t]
    yc = cgate * (cb_ref[...] + cw_ref[0:1] * u2 + cw_ref[1:2] * u1 + cw_ref[2:3] * u)
    msc = jnp.mean(yc * yc, axis=-1, keepdims=True)
    yc_ref[...] = (yc * lax.rsqrt(msc + EPS) * cnw_ref[...]).astype(yc_ref.dtype)


def _mix_in(x2, mod3, norm1_w, w_cat, qnw, knw, conv_w, conv_b, cnw, ones_bd, *, B, L, att_scale, idx_scale):
    n, d = x2.shape
    t = TILE
    nl = L // t
    full = lambda shape: pl.BlockSpec(shape, lambda b, l: (0,) * len(shape))
    tok = lambda w: pl.BlockSpec((t, w), lambda b, l: (b * nl + l, 0))
    out_shapes = (
        jax.ShapeDtypeStruct((B, ATTN_HEADS, 128, L), _MXU),
        jax.ShapeDtypeStruct((n, 512), _MXU),
        jax.ShapeDtypeStruct((B, nl, 512, t), _MXU),
        jax.ShapeDtypeStruct((B, IDX_HEADS, 128, L), _MXU),
        jax.ShapeDtypeStruct((n, 128), _MXU),
        jax.ShapeDtypeStruct((B, IDX_HEADS, L), F32),
        jax.ShapeDtypeStruct((n, 512), _MXU),
    )
    out_specs = (
        pl.BlockSpec((None, ATTN_HEADS, 128, t), lambda b, l: (b, 0, 0, l)),
        tok(512),
        pl.BlockSpec((None, None, 512, t), lambda b, l: (b, l, 0, 0)),
        pl.BlockSpec((None, IDX_HEADS, 128, t), lambda b, l: (b, 0, 0, l)),
        tok(128),
        pl.BlockSpec((None, IDX_HEADS, t), lambda b, l: (b, 0, l)),
        tok(512),
    )
    return pl.pallas_call(
        functools.partial(_mix_in_kernel, att_scale=att_scale, idx_scale=idx_scale),
        grid=(B, nl),
        in_specs=[tok(d),
                  pl.BlockSpec((None, 6, d), lambda b, l: (b, 0, 0)),
                  full((1, d)), full(w_cat.shape), full((1, 512)), full((1, 512)),
                  full((3, 512)), full((1, 512)), full((1, 512)), full((512, 512))],
        out_specs=out_specs,
        out_shape=out_shapes,
        scratch_shapes=[pltpu.VMEM((8, 512), F32)],
        compiler_params=_cparams(("arbitrary", "arbitrary")),
        name="mix_in",
    )(x2, mod3, norm1_w, w_cat, qnw, knw, conv_w, conv_b, cnw, ones_bd)


def _dsa_kernel(qT_ref, iqT_ref, iwT_ref, k_ref, vT_ref, ik_ref, anw_ref, qnw_ref, knw_ref, o_ref,
                I_ref, m_ref, l_ref, acc_ref, bias_ref, s_ref, *, topk, seq_len, att_scale):
    t = DSA_TQ
    kc = TILE
    ndiag = t // kc
    qb = pl.program_id(1)
    nk = (qb + 1) * ndiag
    kf = float(topk)
    inf = float("inf")
    qpos = qb * t + lax.broadcasted_iota(jnp.int32, (1, t), 1)
    rowi = lax.broadcasted_iota(jnp.int32, (kc, 1), 0)

    def scores(c):
        ikc = ik_ref[c]
        acc = jnp.zeros((kc, t), F32)
        for h in range(IDX_HEADS):
            r = _dot(ikc, iqT_ref[h])
            acc = acc + iwT_ref[h:h + 1, :] * jnp.maximum(r, 0.0)
        return acc

    def p1(c, carry):
        mn, mx = carry
        acc = scores(c)
        I_ref[c] = acc
        return (jnp.minimum(mn, jnp.min(acc, axis=0, keepdims=True)),
                jnp.maximum(mx, jnp.max(acc, axis=0, keepdims=True)))

    mn, mx = lax.fori_loop(0, nk - ndiag, p1, (jnp.full((1, t), inf, F32), jnp.full((1, t), -inf, F32)))
    for dchunk in range(ndiag):
        c = nk - ndiag + dchunk
        acc = scores(c)
        causal = (c * kc + rowi) <= qpos
        I_ref[c] = jnp.where(causal, acc, -inf)
        mn = jnp.minimum(mn, jnp.min(jnp.where(causal, acc, inf), axis=0, keepdims=True))
        mx = jnp.maximum(mx, jnp.max(jnp.where(causal, acc, -inf), axis=0, keepdims=True))

    nk2 = (nk + 1) // 2
    nkg = (nk + SCAN_GROUP - 1) // SCAN_GROUP
    for d in range(SCAN_GROUP - 1):
        @pl.when(nk + d < nkg * SCAN_GROUP)
        def _(d=d):
            I_ref[nk + d] = jnp.full((kc, t), -inf, F32)

    slab = 32
    per_chunk = kc // slab

    def scan_pairs(fn, init):
        def body(j, s):
            for m in range(SCAN_GROUP * per_chunk):
                u, r = divmod(m, per_chunk)
                blk = I_ref[SCAN_GROUP * j + u, r * slab:(r + 1) * slab, :]
                s = fn(s, blk, (SCAN_GROUP * j + u) * kc + r * slab)
            return s
        return lax.fori_loop(0, nkg, body, init)

    row_s = lax.broadcasted_iota(jnp.int32, (slab, 1), 0)

    def count(pred):
        s = scan_pairs(lambda s, blk, base: jnp.where(pred(blk, base + row_s), s + 1.0, s),
                       jnp.zeros((slab, t), F32))
        return jnp.sum(s, axis=0, keepdims=True)

    def min_where(pred):
        s = scan_pairs(lambda s, blk, base: jnp.minimum(s, jnp.where(pred(blk), blk, inf)),
                       jnp.full((slab, t), inf, F32))
        return jnp.min(s, axis=0, keepdims=True)
    done0 = jnp.where(qpos + 1 > topk, 0.0, 1.0)
    tau0 = jnp.full((1, t), -inf, F32)
    hi0 = mx + (jnp.abs(mx) + 1.0)

    def cond_a(st):
        it, _, _, _, done = st
        return jnp.logical_and(it < 40, jnp.min(done) < 0.5)

    def body_a(st):
        it, lo, hi, tau, done = st
        mid = lo + (hi - lo) * 0.5
        c = count(lambda blk, _: blk >= mid)
        hit = jnp.logical_and(c == kf, done < 0.5)
        tau = jnp.where(hit, mid, tau)
        done = jnp.where(hit, 1.0, done)
        ge = c >= kf
        return it + 1, jnp.where(ge, mid, lo), jnp.where(ge, hi, mid), tau, done

    n_blind = jnp.where(qb * t + t > topk, BISECT_BLIND_STEPS, 0)
    st = lax.fori_loop(0, n_blind, lambda _, s: body_a(s), (jnp.int32(0), mn, hi0, tau0, done0))
    _, lo, hi, tau, done = lax.while_loop(cond_a, body_a, st)

    def cond_b(st):
        it, _, _, _, done, _, _ = st
        return jnp.logical_and(it < 4096, jnp.min(done) < 0.5)

    def body_b(st):
        it, lo, hi, tau, done, tie, need = st
        vlo = min_where(lambda blk: blk >= lo)
        ngt = count(lambda blk, _: blk > vlo)
        is_tie = jnp.logical_and(ngt < kf, done < 0.5)
        tau = jnp.where(is_tie, vlo, tau)
        need = jnp.where(is_tie, kf - ngt, need)
        tie = jnp.where(is_tie, 1.0, tie)
        done = jnp.where(is_tie, 1.0, done)
        lo2 = min_where(lambda blk: blk > vlo)
        mid = lo2 + (hi - lo2) * 0.5
        c = count(lambda blk, _: blk >= mid)
        hit = jnp.logical_and(c == kf, done < 0.5)
        tau = jnp.where(hit, mid, tau)
        done = jnp.where(hit, 1.0, done)
        ge = c >= kf
        return it + 1, jnp.where(ge, mid, lo2), jnp.where(ge, hi, mid), tau, done, tie, need

    zeros = jnp.zeros((1, t), F32)
    _, _, _, tau, done, tie, need = lax.while_loop(
        cond_b, body_b, (jnp.int32(0), lo, hi, tau, done, zeros, zeros))

    nbits = int(seq_len).bit_length() + 1
    ntrip = jnp.where(jnp.max(tie) > 0.5, nbits, 0)

    def body_j(_, st):
        jlo, jhi = st
        jm = (jlo + jhi) >> 1
        c = count(lambda blk, kpos: jnp.logical_and(blk == tau, kpos <= jm))
        ok = c >= need
        return jnp.where(ok, jlo, jm), jnp.where(ok, jm, jhi)

    _, jhi = lax.fori_loop(0, ntrip, body_j,
                           (jnp.full((1, t), -1, jnp.int32), jnp.full((1, t), seq_len - 1, jnp.int32)))
    jstar = jnp.where(tie > 0.5, jhi, -1)

    ones_rows = jnp.ones((16, kc), _MXU)

    @pl.when(ntrip > 0)
    def _():
        def drop_tail(c, carry):
            blk = I_ref[c]
            drop = jnp.logical_and(jnp.logical_and(blk == tau, (c * kc + rowi) > jstar), tie > 0.5)
            I_ref[c] = jnp.where(drop, -inf, blk)
            return carry
        lax.fori_loop(0, nk, drop_tail, 0)

    tau_eff = jnp.where(done0 > 0.5, -3.0e38, tau)

    def selected(c):
        return I_ref[c] >= tau_eff

    mb = (HEAD_DIM * att_scale) * (jnp.max(jnp.abs(qnw_ref[...]), axis=1, keepdims=True)
                                   * jnp.max(jnp.abs(knw_ref[...]), axis=1, keepdims=True))
    l_ref[...] = jnp.zeros(l_ref.shape, F32)
    acc_ref[...] = jnp.zeros(acc_ref.shape, F32)

    lane_w = 256

    def p3_bounded(j, carry):
        for u in range(2):
            c = 2 * j + u
            bias_ref[u] = jnp.where(selected(c), -mb, NEG)
            for half in range(t // lane_w):
                lanes = slice(half * lane_w, (half + 1) * lane_w)
                ls = []
                for h in range(ATTN_HEADS):
                    p = h // 2
                    rows = slice(h * HEAD_DIM, (h + 1) * HEAD_DIM)
                    s = _dot(k_ref[c, :, p * 128:(p + 1) * 128], qT_ref[h, :, lanes]) + bias_ref[u, :, lanes]
                    pb = jnp.exp2(s).astype(_MXU)
                    ls.append(_dot(ones_rows, pb)[0:1])
                    acc_ref[rows, lanes] += _dot(vT_ref[c, rows, :], pb)
                l_ref[:, lanes] += jnp.concatenate(ls, axis=0)
        return carry

    lax.fori_loop(0, nk2, p3_bounded, 0)

    underflow = jnp.logical_not(jnp.min(l_ref[...]) > 2.0 ** -60)

    def p3(c, carry):
        bias_ref[0] = jnp.where(selected(c), 0.0, NEG)
        mcs = []
        for h in range(ATTN_HEADS):
            p = h // 2
            s = _dot(k_ref[c, :, p * 128:(p + 1) * 128], qT_ref[h]) + bias_ref[0]
            s_ref[h] = s
            mcs.append(jnp.max(s, axis=0, keepdims=True))
        m_old = m_ref[...]
        m_new = jnp.maximum(m_old, jnp.concatenate(mcs, axis=0))
        alpha = jnp.exp2(m_old - m_new)
        m_ref[...] = m_new
        ls = []
        for h in range(ATTN_HEADS):
            rows = slice(h * HEAD_DIM, (h + 1) * HEAD_DIM)
            pb = jnp.exp2(s_ref[h] - m_new[h:h + 1]).astype(_MXU)
            ls.append(_dot(ones_rows, pb)[0:1])
            acc_ref[rows, :] = alpha[h:h + 1] * acc_ref[rows, :] + _dot(vT_ref[c, rows, :], pb)
        l_ref[...] = alpha * l_ref[...] + jnp.concatenate(ls, axis=0)
        return carry

    @pl.when(underflow)
    def _():
        m_ref[...] = jnp.full(m_ref.shape, NEG, F32)
        l_ref[...] = jnp.zeros(l_ref.shape, F32)
        acc_ref[...] = jnp.zeros(acc_ref.shape, F32)
        lax.fori_loop(0, nk, p3, 0)

    for h in range(ATTN_HEADS):
        acc_ref[h * HEAD_DIM:(h + 1) * HEAD_DIM, :] = acc_ref[h * HEAD_DIM:(h + 1) * HEAD_DIM, :] / l_ref[h:h + 1, :]
    yt = acc_ref[...]
    ms = jnp.mean(yt * yt, axis=0, keepdims=True)
    o_ref[...] = (yt * lax.rsqrt(ms + EPS) * anw_ref[...]).T.astype(o_ref.dtype)


def _dsa(qT, iqT, iwT, k4, vT4, ik4, anw_col, qnw, knw, *, B, L, topk, att_scale):
    t = DSA_TQ
    kc = TILE
    nl = L // kc
    nq = L // t
    n = B * L
    return pl.pallas_call(
        functools.partial(_dsa_kernel, topk=topk, seq_len=L, att_scale=att_scale),
        grid=(B, nq),
        in_specs=[pl.BlockSpec((None, ATTN_HEADS, 128, t), lambda b, q: (b, 0, 0, q)),
                  pl.BlockSpec((None, IDX_HEADS, 128, t), lambda b, q: (b, 0, 0, q)),
                  pl.BlockSpec((None, IDX_HEADS, t), lambda b, q: (b, 0, q)),
                  pl.BlockSpec((None, nl, kc, 512), lambda b, q: (b, 0, 0, 0), pipeline_mode=pl.Buffered(1)),
                  pl.BlockSpec((None, nl, 512, kc), lambda b, q: (b, 0, 0, 0), pipeline_mode=pl.Buffered(1)),
                  pl.BlockSpec((None, nl, kc, 128), lambda b, q: (b, 0, 0, 0), pipeline_mode=pl.Buffered(1)),
                  pl.BlockSpec((512, 1), lambda b, q: (0, 0)),
                  pl.BlockSpec((1, 512), lambda b, q: (0, 0)),
                  pl.BlockSpec((1, 512), lambda b, q: (0, 0))],
        out_specs=pl.BlockSpec((t, 512), lambda b, q: (b * nq + q, 0)),
        out_shape=jax.ShapeDtypeStruct((n, 512), _MXU),
        scratch_shapes=[pltpu.VMEM((nl, kc, t), F32),
                        pltpu.VMEM((ATTN_HEADS, t), F32),
                        pltpu.VMEM((ATTN_HEADS, t), F32),
                        pltpu.VMEM((512, t), F32),
                        pltpu.VMEM((2, kc, t), F32),
                        pltpu.VMEM((ATTN_HEADS, kc, t), F32)],
        compiler_params=_cparams(("arbitrary", "arbitrary")),
        name="dsa",
    )(qT, iqT, iwT, k4, vT4, ik4, anw_col, qnw, knw)


def _mix_out_kernel(x_ref, ya_ref, yc_ref, mod_ref, woa_ref, woc_ref, n2_ref, wqT_ref, sk_ref,
                    x1_ref, h2T_ref, sT_ref):
    proj = _dot(ya_ref[...], woa_ref[...]) + _dot(yc_ref[...], woc_ref[...])
    x1 = x_ref[...] + mod_ref[2:3, :] * proj
    x1_ref[...] = x1
    ms = jnp.mean(x1 * x1, axis=-1, keepdims=True)
    h2 = x1 * lax.rsqrt(ms + EPS) * n2_ref[...] * (1.0 + mod_ref[4:5, :]) + mod_ref[3:4, :]
    h2t = h2.T.astype(_MXU)
    h2T_ref[...] = h2t
    qpt = _dot(wqT_ref[...], h2t).astype(_MXU)
    for hp in range(2 * PEER_HEADS):
        sT_ref[hp] = _dot(sk_ref[hp], qpt[hp * 128:(hp + 1) * 128])


def _mix_out(x2, ya, yc, mod3, woa, woc, norm2_w, wqT, sk, *, B, L):
    n, d = x2.shape
    t = TILE
    nl = L // t
    full = lambda shape: pl.BlockSpec(shape, lambda i: (0,) * len(shape))
    return pl.pallas_call(
        _mix_out_kernel,
        grid=(n // t,),
        in_specs=[pl.BlockSpec((t, d), lambda i: (i, 0)),
                  pl.BlockSpec((t, 512), lambda i: (i, 0)),
                  pl.BlockSpec((t, 512), lambda i: (i, 0)),
                  pl.BlockSpec((None, 6, d), lambda i: (i // nl, 0, 0)),
                  full(woa.shape), full(woc.shape), full((1, d)), full(wqT.shape), full(sk.shape)],
        out_specs=(pl.BlockSpec((t, d), lambda i: (i, 0)),
                   pl.BlockSpec((d, t), lambda i: (0, i)),
                   pl.BlockSpec((2 * PEER_HEADS, PEER_KEYS, t), lambda i: (0, 0, i))),
        out_shape=(jax.ShapeDtypeStruct((n, d), F32),
                   jax.ShapeDtypeStruct((d, n), _MXU),
                   jax.ShapeDtypeStruct((2 * PEER_HEADS, PEER_KEYS, n), F32)),
        compiler_params=_cparams(("arbitrary",)),
        name="mix_out",
    )(x2, ya, yc, mod3, woa, woc, norm2_w, wqT, sk)


def _extract_topk(vals, n_take, tie_break):
    r, t = vals.shape
    rowi = lax.broadcasted_iota(jnp.int32, (r, t), 0) if tie_break else None
    work = vals
    rank = jnp.full((r, t), 99.0, F32)
    taken = []
    for kk in range(n_take):
        m = jnp.max(work, axis=0, keepdims=True)
        sel = work == m
        if tie_break:
            sel = rowi == jnp.min(jnp.where(sel, rowi, r), axis=0, keepdims=True)
        rank = jnp.where(sel, float(kk), rank)
        work = jnp.where(sel, -float("inf"), work)
        taken.append(m)
    return taken, rank


_COMBO_B_LIMIT = {1: 8, 2: 5, 3: 4, 4: 3, 5: 2, 6: 2, 7: 2}


def _route_head(s1, s2, tie_break):
    kk = PEER_TOPK
    ninf = -float("inf")
    v1, rank1 = _extract_topk(s1, kk, tie_break)
    v2, rank2 = _extract_topk(s2, kk, tie_break)
    v1all = jnp.concatenate(v1, axis=0)
    v2all = jnp.concatenate(v2, axis=0)
    sub = lax.broadcasted_iota(jnp.int32, (8, s1.shape[1]), 0)
    pieces = [v1[0] + v2all]
    for a in range(1, 8):
        piece = v1[a] + v2all[0:8]
        if _COMBO_B_LIMIT[a] < 8:
            piece = jnp.where(sub < _COMBO_B_LIMIT[a], piece, ninf)
        pieces.append(piece)
    pieces.append(v1all[8:16] + v2[0])
    combo = jnp.concatenate(pieces, axis=0)
    _, crank = _extract_topk(combo, kk, True)
    selc = jnp.where(crank < float(kk), 1.0, 0.0)
    z = jnp.sum(selc * jnp.exp(combo - combo[0:1]), axis=0, keepdims=True)
    cnts = [jnp.sum(selc[0:16], axis=0, keepdims=True)]
    cnts += [jnp.sum(selc[8 + 8 * a:16 + 8 * a], axis=0, keepdims=True) for a in range(1, 8)]
    cnts += [selc[72 + a:73 + a] for a in range(8)]
    cnt1 = jnp.zeros_like(s1)
    for a in range(kk):
        cnt1 = cnt1 + jnp.where(rank1 == float(a), cnts[a], 0.0)
    f2 = jnp.where(rank2 < float(kk), jnp.exp(s2 - v2[0]), 0.0)
    e1 = jnp.where(rank1 < float(kk), jnp.exp(s1 - v1[0]), 0.0) / z
    n_ranked = (jnp.sum(jnp.where(rank1 < float(kk), 1.0, 0.0), axis=0, keepdims=True)
                + jnp.sum(jnp.where(rank2 < float(kk), 1.0, 0.0), axis=0, keepdims=True))
    return rank2, f2, cnt1, e1, n_ranked


def _route_kernel(sT_ref, r2_ref, f2_ref, c1_ref, e1_ref):
    def run(tie_break):
        worst = jnp.zeros((1, sT_ref.shape[2]), F32)
        for h in range(PEER_HEADS):
            rank2, f2, cnt1, e1, n_ranked = _route_head(sT_ref[2 * h], sT_ref[2 * h + 1], tie_break)
            r2_ref[h] = rank2.astype(r2_ref.dtype)
            f2_ref[h] = f2.astype(f2_ref.dtype)
            c1_ref[h] = cnt1
            e1_ref[h] = e1
            worst = jnp.maximum(worst, n_ranked)
        return jnp.max(worst)

    most_ranked = run(False)

    @pl.when(most_ranked > 2.0 * PEER_TOPK)
    def _():
        run(True)


def _route(sT):
    hp, nkeys, n = sT.shape
    t = ROUTE_TM
    spec = pl.BlockSpec((PEER_HEADS, nkeys, t), lambda i: (0, 0, i))
    shp = lambda dt: jax.ShapeDtypeStruct((PEER_HEADS, nkeys, n), dt)
    return pl.pallas_call(
        _route_kernel,
        grid=(n // t,),
        in_specs=[pl.BlockSpec((hp, nkeys, t), lambda i: (0, 0, i))],
        out_specs=(spec, spec, spec, spec),
        out_shape=(shp(_MXU), shp(_MXU), shp(F32), shp(F32)),
        compiler_params=_cparams(("arbitrary",)),
        name="route",
    )(sT)


PEER_TM = 512
PEER_TE = 2048


def _gelu(a):
    c0 = 0.7978845608028654
    inner = a * (c0 + (c0 * 0.044715) * (a * a))
    return (0.5 * a) * (1.0 + jnp.tanh(inner))


def _peer_kernel(h2T_ref, r2_ref, f2_ref, c1_ref, e1_ref, u_ref, vT_ref, x1_ref, mod_ref, o_ref,
                 acc_ref, g_ref):
    e = pl.program_id(1)
    sub = PEER_TE // PEER_KEYS

    @pl.when(e == 0)
    def _():
        acc_ref[...] = jnp.zeros_like(acc_ref)

    for ii in range(sub):
        rows = slice(ii * PEER_KEYS, (ii + 1) * PEER_KEYS)
        a = _dot(u_ref[rows, :], h2T_ref[...]).astype(_MXU)
        i = e * sub + ii
        w = jnp.zeros((PEER_KEYS, PEER_TM), _MXU)
        for h in range(PEER_HEADS):
            c1 = c1_ref[h, pl.ds(i, 1), :].astype(_MXU)
            e1 = e1_ref[h, pl.ds(i, 1), :].astype(_MXU)
            w = w + jnp.where(r2_ref[h] < c1, f2_ref[h], jnp.zeros((), _MXU)) * e1
        g_ref[rows, :] = _gelu(a) * w
    acc_ref[...] += _dot(vT_ref[...], g_ref[...])

    @pl.when(e == pl.num_programs(1) - 1)
    def _():
        o_ref[...] = x1_ref[...] + mod_ref[5:6, :] * acc_ref[...].T


def _peer(h2T, r2, f2, c1, e1, u_b, vT_b, x1, mod3, *, B, L):
    d, n = h2T.shape
    ne = u_b.shape[0]
    tm, te = PEER_TM, PEER_TE
    ntl = L // tm
    rspec = pl.BlockSpec((PEER_HEADS, PEER_KEYS, tm), lambda i, e: (0, 0, i))
    return pl.pallas_call(
        _peer_kernel,
        grid=(n // tm, ne // te),
        in_specs=[pl.BlockSpec((d, tm), lambda i, e: (0, i)),
                  rspec, rspec, rspec, rspec,
                  pl.BlockSpec((te, d), lambda i, e: (e, 0)),
                  pl.BlockSpec((d, te), lambda i, e: (0, e)),
                  pl.BlockSpec((tm, d), lambda i, e: (i, 0)),
                  pl.BlockSpec((None, 6, d), lambda i, e: (i // ntl, 0, 0))],
        out_specs=pl.BlockSpec((tm, d), lambda i, e: (i, 0)),
        out_shape=jax.ShapeDtypeStruct((n, d), F32),
        scratch_shapes=[pltpu.VMEM((d, tm), F32), pltpu.VMEM((te, tm), _MXU)],
        compiler_params=_cparams(("arbitrary", "arbitrary")),
        name="peer",
    )(h2T, r2, f2, c1, e1, u_b, vT_b, x1, mod3)


def kernel(x, c, norm1_w, norm2_w, w_ada, b_ada, w_in, q_norm_w, k_norm_w, conv_w, conv_b,
           attn_out_norm_w, conv_out_norm_w, w_out, peer_wq, peer_subkeys, peer_u, peer_v):
    B, L, D = x.shape
    n = B * L
    assert D == 1024 and L % PEER_TM == 0 and L % (SCAN_GROUP * TILE) == 0 and L % DSA_TQ == 0
    topk = min(MAX_TOPK, L // 4)
    att_scale = HEAD_DIM ** -0.5 * LOG2E
    idx_scale = (IDX_DIM ** -0.5) * (IDX_HEADS ** -0.5)

    c_pad = jnp.pad(c, ((0, 8 - B % 8 if B % 8 else 0), (0, 0)))
    w_ik = w_in[:, 3584:3648]
    w_iw = jnp.pad(w_in[:, 3648:3656], ((0, 0), (0, 120)))
    w_cat = jnp.concatenate([w_in[:, :3584], w_ik, w_ik, w_iw], axis=1).astype(_MXU)
    tile8 = lambda w: jnp.tile(w, ATTN_HEADS).reshape(1, 512)
    head_id = jnp.arange(512) // HEAD_DIM
    ones_bd = (head_id[:, None] == head_id[None, :]).astype(_MXU)
    woa = w_out[:512].astype(_MXU)
    woc = w_out[512:].astype(_MXU)
    wqT = peer_wq.T.astype(_MXU)
    sk = peer_subkeys.reshape(2 * PEER_HEADS, PEER_KEYS, PEER_KEYS).astype(_MXU)
    u_b = peer_u.astype(_MXU)
    vT_b = peer_v.T.astype(_MXU)

    mod = _ada(c_pad, w_ada, b_ada.reshape(1, -1))[:B]
    mod3 = mod.reshape(B, 6, D)
    x2 = x.reshape(n, D)

    qT, k, vT4, iqT, ik, iwT, yc = _mix_in(
        x2, mod3, norm1_w.reshape(1, D), w_cat, tile8(q_norm_w), tile8(k_norm_w),
        conv_w, conv_b.reshape(1, 512), conv_out_norm_w.reshape(1, 512), ones_bd,
        B=B, L=L, att_scale=att_scale, idx_scale=idx_scale)
    nl = L // TILE
    ya = _dsa(qT, iqT, iwT, k.reshape(B, nl, TILE, 512), vT4, ik.reshape(B, nl, TILE, 128),
              attn_out_norm_w.reshape(512, 1), tile8(q_norm_w), tile8(k_norm_w),
              B=B, L=L, topk=topk, att_scale=att_scale)
    x1, h2T, sT = _mix_out(x2, ya, yc, mod3, woa, woc, norm2_w.reshape(1, D), wqT, sk, B=B, L=L)
    r2, f2, c1, e1 = _route(sT)
    out = _peer(h2T, r2, f2, c1, e1, u_b, vT_b, x1, mod3, B=B, L=L)
    return out.reshape(B, L, D)
```

```python
import functools

import jax
import jax.numpy as jnp
from jax import lax
from jax.experimental import pallas as pl
from jax.experimental.pallas import tpu as pltpu

F32 = jnp.float32
_MXU = jnp.bfloat16

EPS = 1e-6
ATTN_HEADS = 8
HEAD_DIM = 64
IDX_HEADS = 8
IDX_DIM = 64
MAX_TOPK = 256
PEER_HEADS = 8
PEER_KEYS = 128
PEER_TOPK = 16
NEG = -1e30
LOG2E = 1.4426950408889634

TILE = 256
DSA_TQ = 256
BISECT_BLIND_STEPS = 19
SCAN_GROUP = 2
VMEM_LIMIT = 56 * 1024 * 1024


def _dot(a, b):
    return jnp.dot(a, b, preferred_element_type=F32)


def _cparams(sem, flags=None):
    return pltpu.CompilerParams(dimension_semantics=sem, vmem_limit_bytes=VMEM_LIMIT, flags=flags)


def _ada_kernel(c_ref, w_ref, b_ref, o_ref):
    c = c_ref[...]
    s = c * jax.nn.sigmoid(c)
    o_ref[...] = _dot(s.astype(_MXU), w_ref[...].astype(_MXU)) + b_ref[...]


def _ada(c_pad, w_ada, b_ada):
    rows, d = c_pad.shape
    n = w_ada.shape[1]
    tn = 1024
    return pl.pallas_call(
        _ada_kernel,
        grid=(n // tn,),
        in_specs=[pl.BlockSpec((rows, d), lambda j: (0, 0)),
                  pl.BlockSpec((d, tn), lambda j: (0, j)),
                  pl.BlockSpec((1, tn), lambda j: (0, j))],
        out_specs=pl.BlockSpec((rows, tn), lambda j: (0, j)),
        out_shape=jax.ShapeDtypeStruct((rows, n), F32),
        compiler_params=_cparams(("arbitrary",)),
        name="ada",
    )(c_pad, w_ada, b_ada)


def _split_pairs(zt, out_ref):
    t = zt.shape[1]
    row = lax.broadcasted_iota(jnp.int32, (128, t), 0)
    for p in range(4):
        pair = zt[p * 128:(p + 1) * 128]
        out_ref[2 * p] = jnp.where(row < 64, pair, 0.0).astype(out_ref.dtype)
        out_ref[2 * p + 1] = jnp.where(row >= 64, pair, 0.0).astype(out_ref.dtype)


def _mix_in_kernel(x_ref, mod_ref, n1_ref, w_ref, qnw_ref, knw_ref, cw_ref, cb_ref, cnw_ref, ones_ref,
                   qT_ref, k_ref, vT_ref, iqT_ref, ik_ref, iwT_ref, yc_ref, prev_ref,
                   *, att_scale, idx_scale):
    li = pl.program_id(1)
    x = x_ref[...]
    t = x.shape[0]
    ms = jnp.mean(x * x, axis=-1, keepdims=True)
    y = x * lax.rsqrt(ms + EPS) * n1_ref[...]
    h = (y * (1.0 + mod_ref[1:2, :]) + mod_ref[0:1, :]).astype(_MXU)

    def proj(c0, c1):
        return _dot(h, w_ref[:, c0:c1])

    ones = ones_ref[...]

    def head_norm(z, w):
        z2 = z * z
        hi = z2.astype(_MXU)
        lo = (z2 - hi.astype(F32)).astype(_MXU)
        ss = _dot(hi, ones) + _dot(lo, ones)
        return z * lax.rsqrt(ss * (1.0 / HEAD_DIM) + EPS) * w

    q = head_norm(proj(0, 512), qnw_ref[...]) * att_scale
    _split_pairs(q.T, qT_ref)
    k = head_norm(proj(512, 1024), knw_ref[...])
    k_ref[...] = k.astype(k_ref.dtype)
    vT_ref[...] = proj(1024, 1536).T.astype(vT_ref.dtype)
    _split_pairs(proj(3072, 3584).T, iqT_ref)
    tail = proj(3584, 3840)
    ik_ref[...] = tail[:, 0:128].astype(ik_ref.dtype)
    iwT_ref[...] = tail[:, 128:256].T[0:IDX_HEADS] * idx_scale

    cgate = proj(1536, 2048)
    u = proj(2048, 2560) * proj(2560, 3072)

    @pl.when(li == 0)
    def _():
        prev_ref[...] = jnp.zeros_like(prev_ref)

    prev = prev_ref[...]
    row = lax.broadcasted_iota(jnp.int32, (t, 1), 0)
    u1 = jnp.where(row == 0, prev[7:8], pltpu.roll(u, 1, 0))
    u2 = jnp.where(row == 0, prev[6:7], jnp.where(row == 1, prev[7:8], pltpu.roll(u, 2, 0)))
    prev_ref[...] = u[t - 8:t]
    yc = cgate * (cb_ref[...] + cw_ref[0:1] * u2 + cw_ref[1:2] * u1 + cw_ref[2:3] * u)
    msc = jnp.mean(yc * yc, axis=-1, keepdims=True)
    yc_ref[...] = (yc * lax.rsqrt(msc + EPS) * cnw_ref[...]).astype(yc_ref.dtype)


def _mix_in(x2, mod3, norm1_w, w_cat, qnw, knw, conv_w, conv_b, cnw, ones_bd, *, B, L, att_scale, idx_scale):
    n, d = x2.shape
    t = TILE
    nl = L // t
    full = lambda shape: pl.BlockSpec(shape, lambda b, l: (0,) * len(shape))
    tok = lambda w: pl.BlockSpec((t, w), lambda b, l: (b * nl + l, 0))
    out_shapes = (
        jax.ShapeDtypeStruct((B, ATTN_HEADS, 128, L), _MXU),
        jax.ShapeDtypeStruct((n, 512), _MXU),
        jax.ShapeDtypeStruct((B, nl, 512, t), _MXU),
        jax.ShapeDtypeStruct((B, IDX_HEADS, 128, L), _MXU),
        jax.ShapeDtypeStruct((n, 128), _MXU),
        jax.ShapeDtypeStruct((B, IDX_HEADS, L), F32),
        jax.ShapeDtypeStruct((n, 512), _MXU),
    )
    out_specs = (
        pl.BlockSpec((None, ATTN_HEADS, 128, t), lambda b, l: (b, 0, 0, l)),
        tok(512),
        pl.BlockSpec((None, None, 512, t), lambda b, l: (b, l, 0, 0)),
        pl.BlockSpec((None, IDX_HEADS, 128, t), lambda b, l: (b, 0, 0, l)),
        tok(128),
        pl.BlockSpec((None, IDX_HEADS, t), lambda b, l: (b, 0, l)),
        tok(512),
    )
    return pl.pallas_call(
        functools.partial(_mix_in_kernel, att_scale=att_scale, idx_scale=idx_scale),
        grid=(B, nl),
        in_specs=[tok(d),
                  pl.BlockSpec((None, 6, d), lambda b, l: (b, 0, 0)),
                  full((1, d)), full(w_cat.shape), full((1, 512)), full((1, 512)),
                  full((3, 512)), full((1, 512)), full((1, 512)), full((512, 512))],
        out_specs=out_specs,
        out_shape=out_shapes,
        scratch_shapes=[pltpu.VMEM((8, 512), F32)],
        compiler_params=_cparams(("arbitrary", "arbitrary")),
        name="mix_in",
    )(x2, mod3, norm1_w, w_cat, qnw, knw, conv_w, conv_b, cnw, ones_bd)


def _dsa_kernel(qT_ref, iqT_ref, iwT_ref, k_ref, vT_ref, ik_ref, anw_ref, qnw_ref, knw_ref, o_ref,
                I_ref, m_ref, l_ref, acc_ref, bias_ref, s_ref, *, topk, seq_len, att_scale):
    t = DSA_TQ
    kc = TILE
    ndiag = t // kc
    qb = pl.program_id(1)
    nk = (qb + 1) * ndiag
    kf = float(topk)
    inf = float("inf")
    qpos = qb * t + lax.broadcasted_iota(jnp.int32, (1, t), 1)
    rowi = lax.broadcasted_iota(jnp.int32, (kc, 1), 0)

    def scores(c):
        ikc = ik_ref[c]
        acc = jnp.zeros((kc, t), F32)
        for h in range(IDX_HEADS):
            r = _dot(ikc, iqT_ref[h])
            acc = acc + iwT_ref[h:h + 1, :] * jnp.maximum(r, 0.0)
        return acc

    def p1(c, carry):
        mn, mx = carry
        acc = scores(c)
        I_ref[c] = acc
        return (jnp.minimum(mn, jnp.min(acc, axis=0, keepdims=True)),
                jnp.maximum(mx, jnp.max(acc, axis=0, keepdims=True)))

    mn, mx = lax.fori_loop(0, nk - ndiag, p1, (jnp.full((1, t), inf, F32), jnp.full((1, t), -inf, F32)))
    for dchunk in range(ndiag):
        c = nk - ndiag + dchunk
        acc = scores(c)
        causal = (c * kc + rowi) <= qpos
        I_ref[c] = jnp.where(causal, acc, -inf)
        mn = jnp.minimum(mn, jnp.min(jnp.where(causal, acc, inf), axis=0, keepdims=True))
        mx = jnp.maximum(mx, jnp.max(jnp.where(causal, acc, -inf), axis=0, keepdims=True))

    nk2 = (nk + 1) // 2
    nkg = (nk + SCAN_GROUP - 1) // SCAN_GROUP
    for d in range(SCAN_GROUP - 1):
        @pl.when(nk + d < nkg * SCAN_GROUP)
        def _(d=d):
            I_ref[nk + d] = jnp.full((kc, t), -inf, F32)

    slab = 32
    per_chunk = kc // slab

    def scan_pairs(fn, init):
        def body(j, s):
            for m in range(SCAN_GROUP * per_chunk):
                u, r = divmod(m, per_chunk)
                blk = I_ref[SCAN_GROUP * j + u, r * slab:(r + 1) * slab, :]
                s = fn(s, blk, (SCAN_GROUP * j + u) * kc + r * slab)
            return s
        return lax.fori_loop(0, nkg, body, init)

    row_s = lax.broadcasted_iota(jnp.int32, (slab, 1), 0)

    def count(pred):
        s = scan_pairs(lambda s, blk, base: jnp.where(pred(blk, base + row_s), s + 1.0, s),
                       jnp.zeros((slab, t), F32))
        return jnp.sum(s, axis=0, keepdims=True)

    def min_where(pred):
        s = scan_pairs(lambda s, blk, base: jnp.minimum(s, jnp.where(pred(blk), blk, inf)),
                       jnp.full((slab, t), inf, F32))
        return jnp.min(s, axis=0, keepdims=True)
    done0 = jnp.where(qpos + 1 > topk, 0.0, 1.0)
    tau0 = jnp.full((1, t), -inf, F32)
    hi0 = mx + (jnp.abs(mx) + 1.0)

    def cond_a(st):
        it, _, _, _, done = st
        return jnp.logical_and(it < 40, jnp.min(done) < 0.5)

    def body_a(st):
        it, lo, hi, tau, done = st
        mid = lo + (hi - lo) * 0.5
        c = count(lambda blk, _: blk >= mid)
        hit = jnp.logical_and(c == kf, done < 0.5)
        tau = jnp.where(hit, mid, tau)
        done = jnp.where(hit, 1.0, done)
        ge = c >= kf
        return it + 1, jnp.where(ge, mid, lo), jnp.where(ge, hi, mid), tau, done

    n_blind = jnp.where(qb * t + t > topk, BISECT_BLIND_STEPS, 0)
    st = lax.fori_loop(0, n_blind, lambda _, s: body_a(s), (jnp.int32(0), mn, hi0, tau0, done0))
    _, lo, hi, tau, done = lax.while_loop(cond_a, body_a, st)

    def cond_b(st):
        it, _, _, _, done, _, _ = st
        return jnp.logical_and(it < 4096, jnp.min(done) < 0.5)

    def body_b(st):
        it, lo, hi, tau, done, tie, need = st
        vlo = min_where(lambda blk: blk >= lo)
        ngt = count(lambda blk, _: blk > vlo)
        is_tie = jnp.logical_and(ngt < kf, done < 0.5)
        tau = jnp.where(is_tie, vlo, tau)
        need = jnp.where(is_tie, kf - ngt, need)
        tie = jnp.where(is_tie, 1.0, tie)
        done = jnp.where(is_tie, 1.0, done)
        lo2 = min_where(lambda blk: blk > vlo)
        mid = lo2 + (hi - lo2) * 0.5
        c = count(lambda blk, _: blk >= mid)
        hit = jnp.logical_and(c == kf, done < 0.5)
        tau = jnp.where(hit, mid, tau)
        done = jnp.where(hit, 1.0, done)
        ge = c >= kf
        return it + 1, jnp.where(ge, mid, lo2), jnp.where(ge, hi, mid), tau, done, tie, need

    zeros = jnp.zeros((1, t), F32)
    _, _, _, tau, done, tie, need = lax.while_loop(
        cond_b, body_b, (jnp.int32(0), lo, hi, tau, done, zeros, zeros))

    nbits = int(seq_len).bit_length() + 1
    ntrip = jnp.where(jnp.max(tie) > 0.5, nbits, 0)

    def body_j(_, st):
        jlo, jhi = st
        jm = (jlo + jhi) >> 1
        c = count(lambda blk, kpos: jnp.logical_and(blk == tau, kpos <= jm))
        ok = c >= need
        return jnp.where(ok, jlo, jm), jnp.where(ok, jm, jhi)

    _, jhi = lax.fori_loop(0, ntrip, body_j,
                           (jnp.full((1, t), -1, jnp.int32), jnp.full((1, t), seq_len - 1, jnp.int32)))
    jstar = jnp.where(tie > 0.5, jhi, -1)

    ones_rows = jnp.ones((16, kc), _MXU)

    @pl.when(ntrip > 0)
    def _():
        def drop_tail(c, carry):
            blk = I_ref[c]
            drop = jnp.logical_and(jnp.logical_and(blk == tau, (c * kc + rowi) > jstar), tie > 0.5)
            I_ref[c] = jnp.where(drop, -inf, blk)
            return carry
        lax.fori_loop(0, nk, drop_tail, 0)

    tau_eff = jnp.where(done0 > 0.5, -3.0e38, tau)

    def selected(c):
        return I_ref[c] >= tau_eff

    mb = (HEAD_DIM * att_scale) * (jnp.max(jnp.abs(qnw_ref[...]), axis=1, keepdims=True)
                                   * jnp.max(jnp.abs(knw_ref[...]), axis=1, keepdims=True))
    l_ref[...] = jnp.zeros(l_ref.shape, F32)
    acc_ref[...] = jnp.zeros(acc_ref.shape, F32)

    lane_w = 256

    def p3_bounded(j, carry):
        for u in range(2):
            c = 2 * j + u
            bias_ref[u] = jnp.where(selected(c), -mb, NEG)
            for half in range(t // lane_w):
                lanes = slice(half * lane_w, (half + 1) * lane_w)
                ls = []
                for h in range(ATTN_HEADS):
                    p = h // 2
                    rows = slice(h * HEAD_DIM, (h + 1) * HEAD_DIM)
                    s = _dot(k_ref[c, :, p * 128:(p + 1) * 128], qT_ref[h, :, lanes]) + bias_ref[u, :, lanes]
                    pb = jnp.exp2(s).astype(_MXU)
                    ls.append(_dot(ones_rows, pb)[0:1])
                    acc_ref[rows, lanes] += _dot(vT_ref[c, rows, :], pb)
                l_ref[:, lanes] += jnp.concatenate(ls, axis=0)
        return carry

    lax.fori_loop(0, nk2, p3_bounded, 0)

    underflow = jnp.logical_not(jnp.min(l_ref[...]) > 2.0 ** -60)

    def p3(c, carry):
        bias_ref[0] = jnp.where(selected(c), 0.0, NEG)
        mcs = []
        for h in range(ATTN_HEADS):
            p = h // 2
            s = _dot(k_ref[c, :, p * 128:(p + 1) * 128], qT_ref[h]) + bias_ref[0]
            s_ref[h] = s
            mcs.append(jnp.max(s, axis=0, keepdims=True))
        m_old = m_ref[...]
        m_new = jnp.maximum(m_old, jnp.concatenate(mcs, axis=0))
        alpha = jnp.exp2(m_old - m_new)
        m_ref[...] = m_new
        ls = []
        for h in range(ATTN_HEADS):
            rows = slice(h * HEAD_DIM, (h + 1) * HEAD_DIM)
            pb = jnp.exp2(s_ref[h] - m_new[h:h + 1]).astype(_MXU)
            ls.append(_dot(ones_rows, pb)[0:1])
            acc_ref[rows, :] = alpha[h:h + 1] * acc_ref[rows, :] + _dot(vT_ref[c, rows, :], pb)
        l_ref[...] = alpha * l_ref[...] + jnp.concatenate(ls, axis=0)
        return carry

    @pl.when(underflow)
    def _():
        m_ref[...] = jnp.full(m_ref.shape, NEG, F32)
        l_ref[...] = jnp.zeros(l_ref.shape, F32)
        acc_ref[...] = jnp.zeros(acc_ref.shape, F32)
        lax.fori_loop(0, nk, p3, 0)

    for h in range(ATTN_HEADS):
        acc_ref[h * HEAD_DIM:(h + 1) * HEAD_DIM, :] = acc_ref[h * HEAD_DIM:(h + 1) * HEAD_DIM, :] / l_ref[h:h + 1, :]
    yt = acc_ref[...]
    ms = jnp.mean(yt * yt, axis=0, keepdims=True)
    o_ref[...] = (yt * lax.rsqrt(ms + EPS) * anw_ref[...]).T.astype(o_ref.dtype)


def _dsa(qT, iqT, iwT, k4, vT4, ik4, anw_col, qnw, knw, *, B, L, topk, att_scale):
    t = DSA_TQ
    kc = TILE
    nl = L // kc
    nq = L // t
    n = B * L
    return pl.pallas_call(
        functools.partial(_dsa_kernel, topk=topk, seq_len=L, att_scale=att_scale),
        grid=(B, nq),
        in_specs=[pl.BlockSpec((None, ATTN_HEADS, 128, t), lambda b, q: (b, 0, 0, q)),
                  pl.BlockSpec((None, IDX_HEADS, 128, t), lambda b, q: (b, 0, 0, q)),
                  pl.BlockSpec((None, IDX_HEADS, t), lambda b, q: (b, 0, q)),
                  pl.BlockSpec((None, nl, kc, 512), lambda b, q: (b, 0, 0, 0), pipeline_mode=pl.Buffered(1)),
                  pl.BlockSpec((None, nl, 512, kc), lambda b, q: (b, 0, 0, 0), pipeline_mode=pl.Buffered(1)),
                  pl.BlockSpec((None, nl, kc, 128), lambda b, q: (b, 0, 0, 0), pipeline_mode=pl.Buffered(1)),
                  pl.BlockSpec((512, 1), lambda b, q: (0, 0)),
                  pl.BlockSpec((1, 512), lambda b, q: (0, 0)),
                  pl.BlockSpec((1, 512), lambda b, q: (0, 0))],
        out_specs=pl.BlockSpec((t, 512), lambda b, q: (b * nq + q, 0)),
        out_shape=jax.ShapeDtypeStruct((n, 512), _MXU),
        scratch_shapes=[pltpu.VMEM((nl, kc, t), F32),
                        pltpu.VMEM((ATTN_HEADS, t), F32),
                        pltpu.VMEM((ATTN_HEADS, t), F32),
                        pltpu.VMEM((512, t), F32),
                        pltpu.VMEM((2, kc, t), F32),
                        pltpu.VMEM((ATTN_HEADS, kc, t), F32)],
        compiler_params=_cparams(("arbitrary", "arbitrary")),
        name="dsa",
    )(qT, iqT, iwT, k4, vT4, ik4, anw_col, qnw, knw)


def _mix_out_kernel(x_ref, ya_ref, yc_ref, mod_ref, woa_ref, woc_ref, n2_ref, wqT_ref, sk_ref,
                    x1_ref, h2T_ref, r2_ref, f2_ref, c1_ref, e1_ref, sT_ref):
    proj = _dot(ya_ref[...], woa_ref[...]) + _dot(yc_ref[...], woc_ref[...])
    x1 = x_ref[...] + mod_ref[2:3, :] * proj
    x1_ref[...] = x1
    ms = jnp.mean(x1 * x1, axis=-1, keepdims=True)
    h2 = x1 * lax.rsqrt(ms + EPS) * n2_ref[...] * (1.0 + mod_ref[4:5, :]) + mod_ref[3:4, :]
    h2t = h2.T.astype(_MXU)
    h2T_ref[...] = h2t
    qpt = _dot(wqT_ref[...], h2t).astype(_MXU)
    for hp in range(2 * PEER_HEADS):
        sT_ref[hp] = _dot(sk_ref[hp], qpt[hp * 128:(hp + 1) * 128])
    _route_tile(sT_ref, r2_ref, f2_ref, c1_ref, e1_ref)


def _mix_out(x2, ya, yc, mod3, woa, woc, norm2_w, wqT, sk, *, B, L):
    n, d = x2.shape
    t = TILE
    nl = L // t
    full = lambda shape: pl.BlockSpec(shape, lambda i: (0,) * len(shape))
    rspec = pl.BlockSpec((PEER_HEADS, PEER_KEYS, t), lambda i: (0, 0, i))
    rshp = lambda dt: jax.ShapeDtypeStruct((PEER_HEADS, PEER_KEYS, n), dt)
    return pl.pallas_call(
        _mix_out_kernel,
        grid=(n // t,),
        in_specs=[pl.BlockSpec((t, d), lambda i: (i, 0)),
                  pl.BlockSpec((t, 512), lambda i: (i, 0)),
                  pl.BlockSpec((t, 512), lambda i: (i, 0)),
                  pl.BlockSpec((None, 6, d), lambda i: (i // nl, 0, 0)),
                  full(woa.shape), full(woc.shape), full((1, d)), full(wqT.shape), full(sk.shape)],
        out_specs=(pl.BlockSpec((t, d), lambda i: (i, 0)),
                   pl.BlockSpec((d, t), lambda i: (0, i)),
                   rspec, rspec, rspec, rspec),
        out_shape=(jax.ShapeDtypeStruct((n, d), F32),
                   jax.ShapeDtypeStruct((d, n), _MXU),
                   rshp(_MXU), rshp(_MXU), rshp(F32), rshp(F32)),
        scratch_shapes=[pltpu.VMEM((2 * PEER_HEADS, PEER_KEYS, t), F32)],
        compiler_params=_cparams(("arbitrary",)),
        name="mix_out",
    )(x2, ya, yc, mod3, woa, woc, norm2_w, wqT, sk)


def _extract_topk(vals, n_take, tie_break):
    r, t = vals.shape
    rowi = lax.broadcasted_iota(jnp.int32, (r, t), 0) if tie_break else None
    work = vals
    rank = jnp.full((r, t), 99.0, F32)
    taken = []
    for kk in range(n_take):
        m = jnp.max(work, axis=0, keepdims=True)
        sel = work == m
        if tie_break:
            sel = rowi == jnp.min(jnp.where(sel, rowi, r), axis=0, keepdims=True)
        rank = jnp.where(sel, float(kk), rank)
        work = jnp.where(sel, -float("inf"), work)
        taken.append(m)
    return taken, rank


_COMBO_B_LIMIT = {1: 8, 2: 5, 3: 4, 4: 3, 5: 2, 6: 2, 7: 2}


def _route_head(s1, s2, tie_break):
    kk = PEER_TOPK
    ninf = -float("inf")
    v1, rank1 = _extract_topk(s1, kk, tie_break)
    v2, rank2 = _extract_topk(s2, kk, tie_break)
    v1all = jnp.concatenate(v1, axis=0)
    v2all = jnp.concatenate(v2, axis=0)
    sub = lax.broadcasted_iota(jnp.int32, (8, s1.shape[1]), 0)
    pieces = [v1[0] + v2all]
    for a in range(1, 8):
        piece = v1[a] + v2all[0:8]
        if _COMBO_B_LIMIT[a] < 8:
            piece = jnp.where(sub < _COMBO_B_LIMIT[a], piece, ninf)
        pieces.append(piece)
    pieces.append(v1all[8:16] + v2[0])
    combo = jnp.concatenate(pieces, axis=0)
    _, crank = _extract_topk(combo, kk, True)
    selc = jnp.where(crank < float(kk), 1.0, 0.0)
    z = jnp.sum(selc * jnp.exp(combo - combo[0:1]), axis=0, keepdims=True)
    cnts = [jnp.sum(selc[0:16], axis=0, keepdims=True)]
    cnts += [jnp.sum(selc[8 + 8 * a:16 + 8 * a], axis=0, keepdims=True) for a in range(1, 8)]
    cnts += [selc[72 + a:73 + a] for a in range(8)]
    cnt1 = jnp.zeros_like(s1)
    for a in range(kk):
        cnt1 = cnt1 + jnp.where(rank1 == float(a), cnts[a], 0.0)
    f2 = jnp.where(rank2 < float(kk), jnp.exp(s2 - v2[0]), 0.0)
    e1 = jnp.where(rank1 < float(kk), jnp.exp(s1 - v1[0]), 0.0) / z
    n_ranked = (jnp.sum(jnp.where(rank1 < float(kk), 1.0, 0.0), axis=0, keepdims=True)
                + jnp.sum(jnp.where(rank2 < float(kk), 1.0, 0.0), axis=0, keepdims=True))
    return rank2, f2, cnt1, e1, n_ranked


def _route_tile(sT_ref, r2_ref, f2_ref, c1_ref, e1_ref):
    def run(tie_break):
        worst = jnp.zeros((1, sT_ref.shape[2]), F32)
        for h in range(PEER_HEADS):
            rank2, f2, cnt1, e1, n_ranked = _route_head(sT_ref[2 * h], sT_ref[2 * h + 1], tie_break)
            r2_ref[h] = rank2.astype(r2_ref.dtype)
            f2_ref[h] = f2.astype(f2_ref.dtype)
            c1_ref[h] = cnt1
            e1_ref[h] = e1
            worst = jnp.maximum(worst, n_ranked)
        return jnp.max(worst)

    most_ranked = run(False)

    @pl.when(most_ranked > 2.0 * PEER_TOPK)
    def _():
        run(True)


PEER_TM = 512
PEER_TE = 2048


def _gelu(a):
    c0 = 0.7978845608028654
    inner = a * (c0 + (c0 * 0.044715) * (a * a))
    return (0.5 * a) * (1.0 + jnp.tanh(inner))


def _peer_kernel(h2T_ref, r2_ref, f2_ref, c1_ref, e1_ref, u_ref, vT_ref, x1_ref, mod_ref, o_ref,
                 acc_ref, g_ref):
    e = pl.program_id(1)
    sub = PEER_TE // PEER_KEYS

    @pl.when(e == 0)
    def _():
        acc_ref[...] = jnp.zeros_like(acc_ref)

    for ii in range(sub):
        rows = slice(ii * PEER_KEYS, (ii + 1) * PEER_KEYS)
        a = _dot(u_ref[rows, :], h2T_ref[...]).astype(_MXU)
        i = e * sub + ii
        w = jnp.zeros((PEER_KEYS, PEER_TM), _MXU)
        for h in range(PEER_HEADS):
            c1 = c1_ref[h, pl.ds(i, 1), :].astype(_MXU)
            e1 = e1_ref[h, pl.ds(i, 1), :].astype(_MXU)
            w = w + jnp.where(r2_ref[h] < c1, f2_ref[h], jnp.zeros((), _MXU)) * e1
        g_ref[rows, :] = _gelu(a) * w
    acc_ref[...] += _dot(vT_ref[...], g_ref[...])

    @pl.when(e == pl.num_programs(1) - 1)
    def _():
        o_ref[...] = x1_ref[...] + mod_ref[5:6, :] * acc_ref[...].T


def _peer(h2T, r2, f2, c1, e1, u_b, vT_b, x1, mod3, *, B, L):
    d, n = h2T.shape
    ne = u_b.shape[0]
    tm, te = PEER_TM, PEER_TE
    ntl = L // tm
    rspec = pl.BlockSpec((PEER_HEADS, PEER_KEYS, tm), lambda i, e: (0, 0, i))
    return pl.pallas_call(
        _peer_kernel,
        grid=(n // tm, ne // te),
        in_specs=[pl.BlockSpec((d, tm), lambda i, e: (0, i)),
                  rspec, rspec, rspec, rspec,
                  pl.BlockSpec((te, d), lambda i, e: (e, 0)),
                  pl.BlockSpec((d, te), lambda i, e: (0, e)),
                  pl.BlockSpec((tm, d), lambda i, e: (i, 0)),
                  pl.BlockSpec((None, 6, d), lambda i, e: (i // ntl, 0, 0))],
        out_specs=pl.BlockSpec((tm, d), lambda i, e: (i, 0)),
        out_shape=jax.ShapeDtypeStruct((n, d), F32),
        scratch_shapes=[pltpu.VMEM((d, tm), F32), pltpu.VMEM((te, tm), _MXU)],
        compiler_params=_cparams(("arbitrary", "arbitrary")),
        name="peer",
    )(h2T, r2, f2, c1, e1, u_b, vT_b, x1, mod3)


def kernel(x, c, norm1_w, norm2_w, w_ada, b_ada, w_in, q_norm_w, k_norm_w, conv_w, conv_b,
           attn_out_norm_w, conv_out_norm_w, w_out, peer_wq, peer_subkeys, peer_u, peer_v):
    B, L, D = x.shape
    n = B * L
    assert D == 1024 and L % PEER_TM == 0 and L % (SCAN_GROUP * TILE) == 0 and L % DSA_TQ == 0
    topk = min(MAX_TOPK, L // 4)
    att_scale = HEAD_DIM ** -0.5 * LOG2E
    idx_scale = (IDX_DIM ** -0.5) * (IDX_HEADS ** -0.5)

    c_pad = jnp.pad(c, ((0, 8 - B % 8 if B % 8 else 0), (0, 0)))
    w_ik = w_in[:, 3584:3648]
    w_iw = jnp.pad(w_in[:, 3648:3656], ((0, 0), (0, 120)))
    w_cat = jnp.concatenate([w_in[:, :3584], w_ik, w_ik, w_iw], axis=1).astype(_MXU)
    tile8 = lambda w: jnp.tile(w, ATTN_HEADS).reshape(1, 512)
    head_id = jnp.arange(512) // HEAD_DIM
    ones_bd = (head_id[:, None] == head_id[None, :]).astype(_MXU)
    woa = w_out[:512].astype(_MXU)
    woc = w_out[512:].astype(_MXU)
    wqT = peer_wq.T.astype(_MXU)
    sk = peer_subkeys.reshape(2 * PEER_HEADS, PEER_KEYS, PEER_KEYS).astype(_MXU)
    u_b = peer_u.astype(_MXU)
    vT_b = peer_v.T.astype(_MXU)

    mod = _ada(c_pad, w_ada, b_ada.reshape(1, -1))[:B]
    mod3 = mod.reshape(B, 6, D)
    x2 = x.reshape(n, D)

    qT, k, vT4, iqT, ik, iwT, yc = _mix_in(
        x2, mod3, norm1_w.reshape(1, D), w_cat, tile8(q_norm_w), tile8(k_norm_w),
        conv_w, conv_b.reshape(1, 512), conv_out_norm_w.reshape(1, 512), ones_bd,
        B=B, L=L, att_scale=att_scale, idx_scale=idx_scale)
    nl = L // TILE
    ya = _dsa(qT, iqT, iwT, k.reshape(B, nl, TILE, 512), vT4, ik.reshape(B, nl, TILE, 128),
              attn_out_norm_w.reshape(512, 1), tile8(q_norm_w), tile8(k_norm_w),
              B=B, L=L, topk=topk, att_scale=att_scale)
    x1, h2T, r2, f2, c1, e1 = _mix_out(x2, ya, yc, mod3, woa, woc, norm2_w.reshape(1, D), wqT, sk, B=B, L=L)
    out = _peer(h2T, r2, f2, c1, e1, u_b, vT_b, x1, mod3, B=B, L=L)
    return out.reshape(B, L, D)
```

```python
import functools

import jax
import jax.numpy as jnp
from jax import lax
from jax.experimental import pallas as pl
from jax.experimental.pallas import tpu as pltpu

F32 = jnp.float32
_MXU = jnp.bfloat16

EPS = 1e-6
ATTN_HEADS = 8
HEAD_DIM = 64
IDX_HEADS = 8
IDX_DIM = 64
MAX_TOPK = 256
PEER_HEADS = 8
PEER_KEYS = 128
PEER_TOPK = 16
NEG = -1e30
LOG2E = 1.4426950408889634

TILE = 256
DSA_TQ = 256
BISECT_BLIND_STEPS = 19
SCAN_GROUP = 2
VMEM_LIMIT = 56 * 1024 * 1024


def _dot(a, b):
    return jnp.dot(a, b, preferred_element_type=F32)


def _cparams(sem, flags=None):
    return pltpu.CompilerParams(dimension_semantics=sem, vmem_limit_bytes=VMEM_LIMIT, flags=flags)


def _ada_kernel(c_ref, w_ref, b_ref, o_ref):
    c = c_ref[...]
    s = c * jax.nn.sigmoid(c)
    o_ref[...] = _dot(s.astype(_MXU), w_ref[...].astype(_MXU)) + b_ref[...]


def _ada(c_pad, w_ada, b_ada):
    rows, d = c_pad.shape
    n = w_ada.shape[1]
    tn = 1024
    return pl.pallas_call(
        _ada_kernel,
        grid=(n // tn,),
        in_specs=[pl.BlockSpec((rows, d), lambda j: (0, 0)),
                  pl.BlockSpec((d, tn), lambda j: (0, j)),
                  pl.BlockSpec((1, tn), lambda j: (0, j))],
        out_specs=pl.BlockSpec((rows, tn), lambda j: (0, j)),
        out_shape=jax.ShapeDtypeStruct((rows, n), F32),
        compiler_params=_cparams(("arbitrary",)),
        name="ada",
    )(c_pad, w_ada, b_ada)


def _split_pairs(zt, out_ref):
    t = zt.shape[1]
    row = lax.broadcasted_iota(jnp.int32, (128, t), 0)
    for p in range(4):
        pair = zt[p * 128:(p + 1) * 128]
        out_ref[2 * p] = jnp.where(row < 64, pair, 0.0).astype(out_ref.dtype)
        out_ref[2 * p + 1] = jnp.where(row >= 64, pair, 0.0).astype(out_ref.dtype)


def _mix_in_kernel(x_ref, mod_ref, n1_ref, w_ref, qnw_ref, knw_ref, cw_ref, cb_ref, cnw_ref, ones_ref,
                   qT_ref, k_ref, vT_ref, iqT_ref, ik_ref, iwT_ref, yc_ref, prev_ref,
                   *, att_scale, idx_scale):
    li = pl.program_id(1)
    x = x_ref[...]
    t = x.shape[0]
    ms = jnp.mean(x * x, axis=-1, keepdims=True)
    y = x * lax.rsqrt(ms + EPS) * n1_ref[...]
    h = (y * (1.0 + mod_ref[1:2, :]) + mod_ref[0:1, :]).astype(_MXU)

    def proj(c0, c1):
        return _dot(h, w_ref[:, c0:c1])

    ones = ones_ref[...]

    def head_norm(z, w):
        z2 = z * z
        hi = z2.astype(_MXU)
        lo = (z2 - hi.astype(F32)).astype(_MXU)
        ss = _dot(hi, ones) + _dot(lo, ones)
        return z * lax.rsqrt(ss * (1.0 / HEAD_DIM) + EPS) * w

    q = head_norm(proj(0, 512), qnw_ref[...]) * att_scale
    _split_pairs(q.T, qT_ref)
    k = head_norm(proj(512, 1024), knw_ref[...])
    k_ref[...] = k.astype(k_ref.dtype)
    vt = proj(1024, 1536).T
    pad = jnp.concatenate([jnp.ones((16, t), F32), jnp.zeros((128 - HEAD_DIM - 16, t), F32)], axis=0)
    for hh in range(ATTN_HEADS):
        vT_ref[hh * 128:(hh + 1) * 128, :] = jnp.concatenate(
            [vt[hh * HEAD_DIM:(hh + 1) * HEAD_DIM], pad], axis=0).astype(vT_ref.dtype)
    _split_pairs(proj(3072, 3584).T, iqT_ref)
    tail = proj(3584, 3840)
    ik_ref[...] = tail[:, 0:128].astype(ik_ref.dtype)
    iwT_ref[...] = tail[:, 128:256].T[0:IDX_HEADS] * idx_scale

    cgate = proj(1536, 2048)
    u = proj(2048, 2560) * proj(2560, 3072)

    @pl.when(li == 0)
    def _():
        prev_ref[...] = jnp.zeros_like(prev_ref)

    prev = prev_ref[...]
    row = lax.broadcasted_iota(jnp.int32, (t, 1), 0)
    u1 = jnp.where(row == 0, prev[7:8], pltpu.roll(u, 1, 0))
    u2 = jnp.where(row == 0, prev[6:7], jnp.where(row == 1, prev[7:8], pltpu.roll(u, 2, 0)))
    prev_ref[...] = u[t - 8:t]
    yc = cgate * (cb_ref[...] + cw_ref[0:1] * u2 + cw_ref[1:2] * u1 + cw_ref[2:3] * u)
    msc = jnp.mean(yc * yc, axis=-1, keepdims=True)
    yc_ref[...] = (yc * lax.rsqrt(msc + EPS) * cnw_ref[...]).astype(yc_ref.dtype)


def _mix_in(x2, mod3, norm1_w, w_cat, qnw, knw, conv_w, conv_b, cnw, ones_bd, *, B, L, att_scale, idx_scale):
    n, d = x2.shape
    t = TILE
    nl = L // t
    full = lambda shape: pl.BlockSpec(shape, lambda b, l: (0,) * len(shape))
    tok = lambda w: pl.BlockSpec((t, w), lambda b, l: (b * nl + l, 0))
    out_shapes = (
        jax.ShapeDtypeStruct((B, ATTN_HEADS, 128, L), _MXU),
        jax.ShapeDtypeStruct((n, 512), _MXU),
        jax.ShapeDtypeStruct((B, nl, ATTN_HEADS * 128, t), _MXU),
        jax.ShapeDtypeStruct((B, IDX_HEADS, 128, L), _MXU),
        jax.ShapeDtypeStruct((n, 128), _MXU),
        jax.ShapeDtypeStruct((B, IDX_HEADS, L), F32),
        jax.ShapeDtypeStruct((n, 512), _MXU),
    )
    out_specs = (
        pl.BlockSpec((None, ATTN_HEADS, 128, t), lambda b, l: (b, 0, 0, l)),
        tok(512),
        pl.BlockSpec((None, None, ATTN_HEADS * 128, t), lambda b, l: (b, l, 0, 0)),
        pl.BlockSpec((None, IDX_HEADS, 128, t), lambda b, l: (b, 0, 0, l)),
        tok(128),
        pl.BlockSpec((None, IDX_HEADS, t), lambda b, l: (b, 0, l)),
        tok(512),
    )
    return pl.pallas_call(
        functools.partial(_mix_in_kernel, att_scale=att_scale, idx_scale=idx_scale),
        grid=(B, nl),
        in_specs=[tok(d),
                  pl.BlockSpec((None, 6, d), lambda b, l: (b, 0, 0)),
                  full((1, d)), full(w_cat.shape), full((1, 512)), full((1, 512)),
                  full((3, 512)), full((1, 512)), full((1, 512)), full((512, 512))],
        out_specs=out_specs,
        out_shape=out_shapes,
        scratch_shapes=[pltpu.VMEM((8, 512), F32)],
        compiler_params=_cparams(("arbitrary", "arbitrary")),
        name="mix_in",
    )(x2, mod3, norm1_w, w_cat, qnw, knw, conv_w, conv_b, cnw, ones_bd)


def _dsa_kernel(qT_ref, iqT_ref, iwT_ref, k_ref, vT_ref, ik_ref, anw_ref, qnw_ref, knw_ref, o_ref,
                I_ref, m_ref, l_ref, acc_ref, bias_ref, s_ref, accw_ref, *, topk, seq_len, att_scale):
    t = DSA_TQ
    kc = TILE
    ndiag = t // kc
    qb = pl.program_id(1)
    nk = (qb + 1) * ndiag
    kf = float(topk)
    inf = float("inf")
    qpos = qb * t + lax.broadcasted_iota(jnp.int32, (1, t), 1)
    rowi = lax.broadcasted_iota(jnp.int32, (kc, 1), 0)

    def scores(c):
        ikc = ik_ref[c]
        acc = jnp.zeros((kc, t), F32)
        for h in range(IDX_HEADS):
            r = _dot(ikc, iqT_ref[h])
            acc = acc + iwT_ref[h:h + 1, :] * jnp.maximum(r, 0.0)
        return acc

    def p1(c, carry):
        mn, mx = carry
        acc = scores(c)
        I_ref[c] = acc
        return (jnp.minimum(mn, jnp.min(acc, axis=0, keepdims=True)),
                jnp.maximum(mx, jnp.max(acc, axis=0, keepdims=True)))

    mn, mx = lax.fori_loop(0, nk - ndiag, p1, (jnp.full((1, t), inf, F32), jnp.full((1, t), -inf, F32)))
    for dchunk in range(ndiag):
        c = nk - ndiag + dchunk
        acc = scores(c)
        causal = (c * kc + rowi) <= qpos
        I_ref[c] = jnp.where(causal, acc, -inf)
        mn = jnp.minimum(mn, jnp.min(jnp.where(causal, acc, inf), axis=0, keepdims=True))
        mx = jnp.maximum(mx, jnp.max(jnp.where(causal, acc, -inf), axis=0, keepdims=True))

    nk2 = (nk + 1) // 2
    nkg = (nk + SCAN_GROUP - 1) // SCAN_GROUP
    for d in range(SCAN_GROUP - 1):
        @pl.when(nk + d < nkg * SCAN_GROUP)
        def _(d=d):
            I_ref[nk + d] = jnp.full((kc, t), -inf, F32)

    slab = 32
    per_chunk = kc // slab

    def scan_pairs(fn, init):
        def body(j, s):
            for m in range(SCAN_GROUP * per_chunk):
                u, r = divmod(m, per_chunk)
                blk = I_ref[SCAN_GROUP * j + u, r * slab:(r + 1) * slab, :]
                s = fn(s, blk, (SCAN_GROUP * j + u) * kc + r * slab)
            return s
        return lax.fori_loop(0, nkg, body, init)

    row_s = lax.broadcasted_iota(jnp.int32, (slab, 1), 0)

    def count(pred):
        s = scan_pairs(lambda s, blk, base: jnp.where(pred(blk, base + row_s), s + 1.0, s),
                       jnp.zeros((slab, t), F32))
        return jnp.sum(s, axis=0, keepdims=True)

    def min_where(pred):
        s = scan_pairs(lambda s, blk, base: jnp.minimum(s, jnp.where(pred(blk), blk, inf)),
                       jnp.full((slab, t), inf, F32))
        return jnp.min(s, axis=0, keepdims=True)
    done0 = jnp.where(qpos + 1 > topk, 0.0, 1.0)
    tau0 = jnp.full((1, t), -inf, F32)
    hi0 = mx + (jnp.abs(mx) + 1.0)

    def cond_a(st):
        it, _, _, _, done = st
        return jnp.logical_and(it < 40, jnp.min(done) < 0.5)

    def body_a(st):
        it, lo, hi, tau, done = st
        mid = lo + (hi - lo) * 0.5
        c = count(lambda blk, _: blk >= mid)
        hit = jnp.logical_and(c == kf, done < 0.5)
        tau = jnp.where(hit, mid, tau)
        done = jnp.where(hit, 1.0, done)
        ge = c >= kf
        return it + 1, jnp.where(ge, mid, lo), jnp.where(ge, hi, mid), tau, done

    n_blind = jnp.where(qb * t + t > topk, BISECT_BLIND_STEPS, 0)
    st = lax.fori_loop(0, n_blind, lambda _, s: body_a(s), (jnp.int32(0), mn, hi0, tau0, done0))
    _, lo, hi, tau, done = lax.while_loop(cond_a, body_a, st)

    def cond_b(st):
        it, _, _, _, done, _, _ = st
        return jnp.logical_and(it < 4096, jnp.min(done) < 0.5)

    def body_b(st):
        it, lo, hi, tau, done, tie, need = st
        vlo = min_where(lambda blk: blk >= lo)
        ngt = count(lambda blk, _: blk > vlo)
        is_tie = jnp.logical_and(ngt < kf, done < 0.5)
        tau = jnp.where(is_tie, vlo, tau)
        need = jnp.where(is_tie, kf - ngt, need)
        tie = jnp.where(is_tie, 1.0, tie)
        done = jnp.where(is_tie, 1.0, done)
        lo2 = min_where(lambda blk: blk > vlo)
        mid = lo2 + (hi - lo2) * 0.5
        c = count(lambda blk, _: blk >= mid)
        hit = jnp.logical_and(c == kf, done < 0.5)
        tau = jnp.where(hit, mid, tau)
        done = jnp.where(hit, 1.0, done)
        ge = c >= kf
        return it + 1, jnp.where(ge, mid, lo2), jnp.where(ge, hi, mid), tau, done, tie, need

    zeros = jnp.zeros((1, t), F32)
    _, _, _, tau, done, tie, need = lax.while_loop(
        cond_b, body_b, (jnp.int32(0), lo, hi, tau, done, zeros, zeros))

    nbits = int(seq_len).bit_length() + 1
    ntrip = jnp.where(jnp.max(tie) > 0.5, nbits, 0)

    def body_j(_, st):
        jlo, jhi = st
        jm = (jlo + jhi) >> 1
        c = count(lambda blk, kpos: jnp.logical_and(blk == tau, kpos <= jm))
        ok = c >= need
        return jnp.where(ok, jlo, jm), jnp.where(ok, jm, jhi)

    _, jhi = lax.fori_loop(0, ntrip, body_j,
                           (jnp.full((1, t), -1, jnp.int32), jnp.full((1, t), seq_len - 1, jnp.int32)))
    jstar = jnp.where(tie > 0.5, jhi, -1)


    @pl.when(ntrip > 0)
    def _():
        def drop_tail(c, carry):
            blk = I_ref[c]
            drop = jnp.logical_and(jnp.logical_and(blk == tau, (c * kc + rowi) > jstar), tie > 0.5)
            I_ref[c] = jnp.where(drop, -inf, blk)
            return carry
        lax.fori_loop(0, nk, drop_tail, 0)

    tau_eff = jnp.where(done0 > 0.5, -3.0e38, tau)

    def selected(c):
        return I_ref[c] >= tau_eff

    mb = (HEAD_DIM * att_scale) * (jnp.max(jnp.abs(qnw_ref[...]), axis=1, keepdims=True)
                                   * jnp.max(jnp.abs(knw_ref[...]), axis=1, keepdims=True))
    accw_ref[...] = jnp.zeros(accw_ref.shape, F32)

    lane_w = 256

    def p3_bounded(j, carry):
        for u in range(2):
            c = 2 * j + u
            bias_ref[u] = jnp.where(selected(c), -mb, NEG)
            for half in range(t // lane_w):
                lanes = slice(half * lane_w, (half + 1) * lane_w)
                for h in range(ATTN_HEADS):
                    p = h // 2
                    hrows = slice(h * 128, (h + 1) * 128)
                    s = _dot(k_ref[c, :, p * 128:(p + 1) * 128], qT_ref[h, :, lanes]) + bias_ref[u, :, lanes]
                    pb = jnp.exp2(s).astype(_MXU)
                    accw_ref[hrows, lanes] += _dot(vT_ref[c, hrows, :], pb)
        return carry

    lax.fori_loop(0, nk2, p3_bounded, 0)
    for h in range(ATTN_HEADS):
        acc_ref[h * HEAD_DIM:(h + 1) * HEAD_DIM, :] = accw_ref[h * 128:h * 128 + HEAD_DIM, :]
        l_ref[h:h + 1, :] = accw_ref[h * 128 + HEAD_DIM:h * 128 + HEAD_DIM + 1, :]

    underflow = jnp.logical_not(jnp.min(l_ref[...]) > 2.0 ** -60)

    def p3(c, carry):
        bias_ref[0] = jnp.where(selected(c), 0.0, NEG)
        mcs = []
        for h in range(ATTN_HEADS):
            p = h // 2
            s = _dot(k_ref[c, :, p * 128:(p + 1) * 128], qT_ref[h]) + bias_ref[0]
            s_ref[h] = s
            mcs.append(jnp.max(s, axis=0, keepdims=True))
        m_old = m_ref[...]
        m_new = jnp.maximum(m_old, jnp.concatenate(mcs, axis=0))
        alpha = jnp.exp2(m_old - m_new)
        m_ref[...] = m_new
        ls = []
        for h in range(ATTN_HEADS):
            rows = slice(h * HEAD_DIM, (h + 1) * HEAD_DIM)
            pb = jnp.exp2(s_ref[h] - m_new[h:h + 1]).astype(_MXU)
            pv = _dot(vT_ref[c, h * 128:(h + 1) * 128, :], pb)
            ls.append(pv[HEAD_DIM:HEAD_DIM + 1])
            acc_ref[rows, :] = alpha[h:h + 1] * acc_ref[rows, :] + pv[0:HEAD_DIM]
        l_ref[...] = alpha * l_ref[...] + jnp.concatenate(ls, axis=0)
        return carry

    @pl.when(underflow)
    def _():
        m_ref[...] = jnp.full(m_ref.shape, NEG, F32)
        l_ref[...] = jnp.zeros(l_ref.shape, F32)
        acc_ref[...] = jnp.zeros(acc_ref.shape, F32)
        lax.fori_loop(0, nk, p3, 0)

    for h in range(ATTN_HEADS):
        acc_ref[h * HEAD_DIM:(h + 1) * HEAD_DIM, :] = acc_ref[h * HEAD_DIM:(h + 1) * HEAD_DIM, :] / l_ref[h:h + 1, :]
    yt = acc_ref[...]
    ms = jnp.mean(yt * yt, axis=0, keepdims=True)
    o_ref[...] = (yt * lax.rsqrt(ms + EPS) * anw_ref[...]).T.astype(o_ref.dtype)


def _dsa(qT, iqT, iwT, k4, vT4, ik4, anw_col, qnw, knw, *, B, L, topk, att_scale):
    t = DSA_TQ
    kc = TILE
    nl = L // kc
    nq = L // t
    n = B * L
    return pl.pallas_call(
        functools.partial(_dsa_kernel, topk=topk, seq_len=L, att_scale=att_scale),
        grid=(B, nq),
        in_specs=[pl.BlockSpec((None, ATTN_HEADS, 128, t), lambda b, q: (b, 0, 0, q)),
                  pl.BlockSpec((None, IDX_HEADS, 128, t), lambda b, q: (b, 0, 0, q)),
                  pl.BlockSpec((None, IDX_HEADS, t), lambda b, q: (b, 0, q)),
                  pl.BlockSpec((None, nl, kc, 512), lambda b, q: (b, 0, 0, 0), pipeline_mode=pl.Buffered(1)),
                  pl.BlockSpec((None, nl, ATTN_HEADS * 128, kc), lambda b, q: (b, 0, 0, 0),
                               pipeline_mode=pl.Buffered(1)),
                  pl.BlockSpec((None, nl, kc, 128), lambda b, q: (b, 0, 0, 0), pipeline_mode=pl.Buffered(1)),
                  pl.BlockSpec((512, 1), lambda b, q: (0, 0)),
                  pl.BlockSpec((1, 512), lambda b, q: (0, 0)),
                  pl.BlockSpec((1, 512), lambda b, q: (0, 0))],
        out_specs=pl.BlockSpec((t, 512), lambda b, q: (b * nq + q, 0)),
        out_shape=jax.ShapeDtypeStruct((n, 512), _MXU),
        scratch_shapes=[pltpu.VMEM((nl, kc, t), F32),
                        pltpu.VMEM((ATTN_HEADS, t), F32),
                        pltpu.VMEM((ATTN_HEADS, t), F32),
                        pltpu.VMEM((512, t), F32),
                        pltpu.VMEM((2, kc, t), F32),
                        pltpu.VMEM((ATTN_HEADS, kc, t), F32),
                        pltpu.VMEM((ATTN_HEADS * 128, t), F32)],
        compiler_params=_cparams(("arbitrary", "arbitrary")),
        name="dsa",
    )(qT, iqT, iwT, k4, vT4, ik4, anw_col, qnw, knw)


def _mix_out_kernel(x_ref, ya_ref, yc_ref, mod_ref, woa_ref, woc_ref, n2_ref, wqT_ref, sk_ref,
                    x1_ref, h2T_ref, r2_ref, f2_ref, c1_ref, e1_ref, sT_ref):
    proj = _dot(ya_ref[...], woa_ref[...]) + _dot(yc_ref[...], woc_ref[...])
    x1 = x_ref[...] + mod_ref[2:3, :] * proj
    x1_ref[...] = x1
    ms = jnp.mean(x1 * x1, axis=-1, keepdims=True)
    h2 = x1 * lax.rsqrt(ms + EPS) * n2_ref[...] * (1.0 + mod_ref[4:5, :]) + mod_ref[3:4, :]
    h2t = h2.T.astype(_MXU)
    h2T_ref[...] = h2t
    qpt = _dot(wqT_ref[...], h2t).astype(_MXU)
    for hp in range(2 * PEER_HEADS):
        sT_ref[hp] = _dot(sk_ref[hp], qpt[hp * 128:(hp + 1) * 128])
    _route_tile(sT_ref, r2_ref, f2_ref, c1_ref, e1_ref)


def _mix_out(x2, ya, yc, mod3, woa, woc, norm2_w, wqT, sk, *, B, L):
    n, d = x2.shape
    t = TILE
    nl = L // t
    full = lambda shape: pl.BlockSpec(shape, lambda i: (0,) * len(shape))
    rspec = pl.BlockSpec((PEER_HEADS, PEER_KEYS, t), lambda i: (0, 0, i))
    rshp = lambda dt: jax.ShapeDtypeStruct((PEER_HEADS, PEER_KEYS, n), dt)
    return pl.pallas_call(
        _mix_out_kernel,
        grid=(n // t,),
        in_specs=[pl.BlockSpec((t, d), lambda i: (i, 0)),
                  pl.BlockSpec((t, 512), lambda i: (i, 0)),
                  pl.BlockSpec((t, 512), lambda i: (i, 0)),
                  pl.BlockSpec((None, 6, d), lambda i: (i // nl, 0, 0)),
                  full(woa.shape), full(woc.shape), full((1, d)), full(wqT.shape), full(sk.shape)],
        out_specs=(pl.BlockSpec((t, d), lambda i: (i, 0)),
                   pl.BlockSpec((d, t), lambda i: (0, i)),
                   rspec, rspec, rspec, rspec),
        out_shape=(jax.ShapeDtypeStruct((n, d), F32),
                   jax.ShapeDtypeStruct((d, n), _MXU),
                   rshp(_MXU), rshp(_MXU), rshp(F32), rshp(F32)),
        scratch_shapes=[pltpu.VMEM((2 * PEER_HEADS, PEER_KEYS, t), F32)],
        compiler_params=_cparams(("arbitrary",)),
        name="mix_out",
    )(x2, ya, yc, mod3, woa, woc, norm2_w, wqT, sk)


def _extract_topk(vals, n_take, tie_break):
    r, t = vals.shape
    rowi = lax.broadcasted_iota(jnp.int32, (r, t), 0) if tie_break else None
    work = vals
    rank = jnp.full((r, t), 99.0, F32)
    taken = []
    for kk in range(n_take):
        m = jnp.max(work, axis=0, keepdims=True)
        sel = work == m
        if tie_break:
            sel = rowi == jnp.min(jnp.where(sel, rowi, r), axis=0, keepdims=True)
        rank = jnp.where(sel, float(kk), rank)
        work = jnp.where(sel, -float("inf"), work)
        taken.append(m)
    return taken, rank


_COMBO_B_LIMIT = {1: 8, 2: 5, 3: 4, 4: 3, 5: 2, 6: 2, 7: 2}


def _route_head(s1, s2, tie_break):
    kk = PEER_TOPK
    ninf = -float("inf")
    v1, rank1 = _extract_topk(s1, kk, tie_break)
    v2, rank2 = _extract_topk(s2, kk, tie_break)
    v1all = jnp.concatenate(v1, axis=0)
    v2all = jnp.concatenate(v2, axis=0)
    sub = lax.broadcasted_iota(jnp.int32, (8, s1.shape[1]), 0)
    pieces = [v1[0] + v2all]
    for a in range(1, 8):
        piece = v1[a] + v2all[0:8]
        if _COMBO_B_LIMIT[a] < 8:
            piece = jnp.where(sub < _COMBO_B_LIMIT[a], piece, ninf)
        pieces.append(piece)
    pieces.append(v1all[8:16] + v2[0])
    combo = jnp.concatenate(pieces, axis=0)
    _, crank = _extract_topk(combo, kk, True)
    selc = jnp.where(crank < float(kk), 1.0, 0.0)
    z = jnp.sum(selc * jnp.exp(combo - combo[0:1]), axis=0, keepdims=True)
    cnts = [jnp.sum(selc[0:16], axis=0, keepdims=True)]
    cnts += [jnp.sum(selc[8 + 8 * a:16 + 8 * a], axis=0, keepdims=True) for a in range(1, 8)]
    cnts += [selc[72 + a:73 + a] for a in range(8)]
    cnt1 = jnp.zeros_like(s1)
    for a in range(kk):
        cnt1 = cnt1 + jnp.where(rank1 == float(a), cnts[a], 0.0)
    f2 = jnp.where(rank2 < float(kk), jnp.exp(s2 - v2[0]), 0.0)
    e1 = jnp.where(rank1 < float(kk), jnp.exp(s1 - v1[0]), 0.0) / z
    n_ranked = (jnp.sum(jnp.where(rank1 < float(kk), 1.0, 0.0), axis=0, keepdims=True)
                + jnp.sum(jnp.where(rank2 < float(kk), 1.0, 0.0), axis=0, keepdims=True))
    return rank2, f2, cnt1, e1, n_ranked


def _route_tile(sT_ref, r2_ref, f2_ref, c1_ref, e1_ref):
    def run(tie_break):
        worst = jnp.zeros((1, sT_ref.shape[2]), F32)
        for h in range(PEER_HEADS):
            rank2, f2, cnt1, e1, n_ranked = _route_head(sT_ref[2 * h], sT_ref[2 * h + 1], tie_break)
            r2_ref[h] = rank2.astype(r2_ref.dtype)
            f2_ref[h] = f2.astype(f2_ref.dtype)
            c1_ref[h] = cnt1
            e1_ref[h] = e1
            worst = jnp.maximum(worst, n_ranked)
        return jnp.max(worst)

    most_ranked = run(False)

    @pl.when(most_ranked > 2.0 * PEER_TOPK)
    def _():
        run(True)


PEER_TM = 512
PEER_TE = 2048


def _gelu(a):
    c0 = 0.7978845608028654
    inner = a * (c0 + (c0 * 0.044715) * (a * a))
    return (0.5 * a) * (1.0 + jnp.tanh(inner))


def _peer_kernel(h2T_ref, r2_ref, f2_ref, c1_ref, e1_ref, u_ref, vT_ref, x1_ref, mod_ref, o_ref,
                 acc_ref, g_ref):
    e = pl.program_id(1)
    sub = PEER_TE // PEER_KEYS

    @pl.when(e == 0)
    def _():
        acc_ref[...] = jnp.zeros_like(acc_ref)

    for ii in range(sub):
        rows = slice(ii * PEER_KEYS, (ii + 1) * PEER_KEYS)
        a = _dot(u_ref[rows, :], h2T_ref[...]).astype(_MXU)
        i = e * sub + ii
        w = jnp.zeros((PEER_KEYS, PEER_TM), _MXU)
        for h in range(PEER_HEADS):
            c1 = c1_ref[h, pl.ds(i, 1), :].astype(_MXU)
            e1 = e1_ref[h, pl.ds(i, 1), :].astype(_MXU)
            w = w + jnp.where(r2_ref[h] < c1, f2_ref[h], jnp.zeros((), _MXU)) * e1
        g_ref[rows, :] = _gelu(a) * w
    acc_ref[...] += _dot(vT_ref[...], g_ref[...])

    @pl.when(e == pl.num_programs(1) - 1)
    def _():
        o_ref[...] = x1_ref[...] + mod_ref[5:6, :] * acc_ref[...].T


def _peer(h2T, r2, f2, c1, e1, u_b, vT_b, x1, mod3, *, B, L):
    d, n = h2T.shape
    ne = u_b.shape[0]
    tm, te = PEER_TM, PEER_TE
    ntl = L // tm
    rspec = pl.BlockSpec((PEER_HEADS, PEER_KEYS, tm), lambda i, e: (0, 0, i))
    return pl.pallas_call(
        _peer_kernel,
        grid=(n // tm, ne // te),
        in_specs=[pl.BlockSpec((d, tm), lambda i, e: (0, i)),
                  rspec, rspec, rspec, rspec,
                  pl.BlockSpec((te, d), lambda i, e: (e, 0)),
                  pl.BlockSpec((d, te), lambda i, e: (0, e)),
                  pl.BlockSpec((tm, d), lambda i, e: (i, 0)),
                  pl.BlockSpec((None, 6, d), lambda i, e: (i // ntl, 0, 0))],
        out_specs=pl.BlockSpec((tm, d), lambda i, e: (i, 0)),
        out_shape=jax.ShapeDtypeStruct((n, d), F32),
        scratch_shapes=[pltpu.VMEM((d, tm), F32), pltpu.VMEM((te, tm), _MXU)],
        compiler_params=_cparams(("arbitrary", "arbitrary")),
        name="peer",
    )(h2T, r2, f2, c1, e1, u_b, vT_b, x1, mod3)


def kernel(x, c, norm1_w, norm2_w, w_ada, b_ada, w_in, q_norm_w, k_norm_w, conv_w, conv_b,
           attn_out_norm_w, conv_out_norm_w, w_out, peer_wq, peer_subkeys, peer_u, peer_v):
    B, L, D = x.shape
    n = B * L
    assert D == 1024 and L % PEER_TM == 0 and L % (SCAN_GROUP * TILE) == 0 and L % DSA_TQ == 0
    topk = min(MAX_TOPK, L // 4)
    att_scale = HEAD_DIM ** -0.5 * LOG2E
    idx_scale = (IDX_DIM ** -0.5) * (IDX_HEADS ** -0.5)

    c_pad = jnp.pad(c, ((0, 8 - B % 8 if B % 8 else 0), (0, 0)))
    w_ik = w_in[:, 3584:3648]
    w_iw = jnp.pad(w_in[:, 3648:3656], ((0, 0), (0, 120)))
    w_cat = jnp.concatenate([w_in[:, :3584], w_ik, w_ik, w_iw], axis=1).astype(_MXU)
    tile8 = lambda w: jnp.tile(w, ATTN_HEADS).reshape(1, 512)
    head_id = jnp.arange(512) // HEAD_DIM
    ones_bd = (head_id[:, None] == head_id[None, :]).astype(_MXU)
    woa = w_out[:512].astype(_MXU)
    woc = w_out[512:].astype(_MXU)
    wqT = peer_wq.T.astype(_MXU)
    sk = peer_subkeys.reshape(2 * PEER_HEADS, PEER_KEYS, PEER_KEYS).astype(_MXU)
    u_b = peer_u.astype(_MXU)
    vT_b = peer_v.T.astype(_MXU)

    mod = _ada(c_pad, w_ada, b_ada.reshape(1, -1))[:B]
    mod3 = mod.reshape(B, 6, D)
    x2 = x.reshape(n, D)

    qT, k, vT4, iqT, ik, iwT, yc = _mix_in(
        x2, mod3, norm1_w.reshape(1, D), w_cat, tile8(q_norm_w), tile8(k_norm_w),
        conv_w, conv_b.reshape(1, 512), conv_out_norm_w.reshape(1, 512), ones_bd,
        B=B, L=L, att_scale=att_scale, idx_scale=idx_scale)
    nl = L // TILE
    ya = _dsa(qT, iqT, iwT, k.reshape(B, nl, TILE, 512), vT4, ik.reshape(B, nl, TILE, 128),
              attn_out_norm_w.reshape(512, 1), tile8(q_norm_w), tile8(k_norm_w),
              B=B, L=L, topk=topk, att_scale=att_scale)
    x1, h2T, r2, f2, c1, e1 = _mix_out(x2, ya, yc, mod3, woa, woc, norm2_w.reshape(1, D), wqT, sk, B=B, L=L)
    out = _peer(h2T, r2, f2, c1, e1, u_b, vT_b, x1, mod3, B=B, L=L)
    return out.reshape(B, L, D)
```

```python
import functools

import jax
import jax.numpy as jnp
from jax import lax
from jax.experimental import pallas as pl
from jax.experimental.pallas import tpu as pltpu

F32 = jnp.float32
_MXU = jnp.bfloat16

EPS = 1e-6
ATTN_HEADS = 8
HEAD_DIM = 64
IDX_HEADS = 8
IDX_DIM = 64
MAX_TOPK = 256
PEER_HEADS = 8
PEER_KEYS = 128
PEER_TOPK = 16
NEG = -1e30
LOG2E = 1.4426950408889634

TILE = 256
DSA_TQ = 256
BISECT_BLIND_STEPS = 19
SCAN_GROUP = 2
VMEM_LIMIT = 56 * 1024 * 1024


def _dot(a, b):
    return jnp.dot(a, b, preferred_element_type=F32)


def _cparams(sem, flags=None):
    return pltpu.CompilerParams(dimension_semantics=sem, vmem_limit_bytes=VMEM_LIMIT, flags=flags)


def _ada_kernel(c_ref, w_ref, b_ref, o_ref):
    c = c_ref[...]
    s = c * jax.nn.sigmoid(c)
    o_ref[...] = _dot(s.astype(_MXU), w_ref[...].astype(_MXU)) + b_ref[...]


def _ada(c_pad, w_ada, b_ada):
    rows, d = c_pad.shape
    n = w_ada.shape[1]
    tn = 1024
    return pl.pallas_call(
        _ada_kernel,
        grid=(n // tn,),
        in_specs=[pl.BlockSpec((rows, d), lambda j: (0, 0)),
                  pl.BlockSpec((d, tn), lambda j: (0, j)),
                  pl.BlockSpec((1, tn), lambda j: (0, j))],
        out_specs=pl.BlockSpec((rows, tn), lambda j: (0, j)),
        out_shape=jax.ShapeDtypeStruct((rows, n), F32),
        compiler_params=_cparams(("arbitrary",)),
        name="ada",
    )(c_pad, w_ada, b_ada)


def _split_pairs(zt, out_ref):
    t = zt.shape[1]
    row = lax.broadcasted_iota(jnp.int32, (128, t), 0)
    for p in range(4):
        pair = zt[p * 128:(p + 1) * 128]
        out_ref[2 * p] = jnp.where(row < 64, pair, 0.0).astype(out_ref.dtype)
        out_ref[2 * p + 1] = jnp.where(row >= 64, pair, 0.0).astype(out_ref.dtype)


def _mix_in_kernel(x_ref, mod_ref, n1_ref, w_ref, qnw_ref, knw_ref, cw_ref, cb_ref, cnw_ref, ones_ref,
                   qT_ref, k_ref, vT_ref, iqT_ref, ik_ref, iwT_ref, yc_ref, prev_ref,
                   *, att_scale, idx_scale):
    li = pl.program_id(1)
    x = x_ref[...]
    t = x.shape[0]
    ms = jnp.mean(x * x, axis=-1, keepdims=True)
    y = x * lax.rsqrt(ms + EPS) * n1_ref[...]
    h = (y * (1.0 + mod_ref[1:2, :]) + mod_ref[0:1, :]).astype(_MXU)

    def proj(c0, c1):
        return _dot(h, w_ref[:, c0:c1])

    ones = ones_ref[...]

    def head_norm(z, w):
        z2 = z * z
        hi = z2.astype(_MXU)
        lo = (z2 - hi.astype(F32)).astype(_MXU)
        ss = _dot(hi, ones) + _dot(lo, ones)
        return z * lax.rsqrt(ss * (1.0 / HEAD_DIM) + EPS) * w

    q = head_norm(proj(0, 512), qnw_ref[...]) * att_scale
    _split_pairs(q.T, qT_ref)
    k = head_norm(proj(512, 1024), knw_ref[...])
    k_ref[...] = k.astype(k_ref.dtype)
    vT_ref[...] = proj(1024, 1536).T.astype(vT_ref.dtype)
    _split_pairs(proj(3072, 3584).T, iqT_ref)
    tail = proj(3584, 3840)
    ik_ref[...] = tail[:, 0:128].astype(ik_ref.dtype)
    iwT_ref[...] = tail[:, 128:256].T[0:IDX_HEADS] * idx_scale

    cgate = proj(1536, 2048)
    u = proj(2048, 2560) * proj(2560, 3072)

    @pl.when(li == 0)
    def _():
        prev_ref[...] = jnp.zeros_like(prev_ref)

    prev = prev_ref[...]
    row = lax.broadcasted_iota(jnp.int32, (t, 1), 0)
    u1 = jnp.where(row == 0, prev[7:8], pltpu.roll(u, 1, 0))
    u2 = jnp.where(row == 0, prev[6:7], jnp.where(row == 1, prev[7:8], pltpu.roll(u, 2, 0)))
    prev_ref[...] = u[t - 8:t]
    yc = cgate * (cb_ref[...] + cw_ref[0:1] * u2 + cw_ref[1:2] * u1 + cw_ref[2:3] * u)
    msc = jnp.mean(yc * yc, axis=-1, keepdims=True)
    yc_ref[...] = (yc * lax.rsqrt(msc + EPS) * cnw_ref[...]).astype(yc_ref.dtype)


def _mix_in(x2, mod3, norm1_w, w_cat, qnw, knw, conv_w, conv_b, cnw, ones_bd, *, B, L, att_scale, idx_scale):
    n, d = x2.shape
    t = TILE
    nl = L // t
    full = lambda shape: pl.BlockSpec(shape, lambda b, l: (0,) * len(shape))
    tok = lambda w: pl.BlockSpec((t, w), lambda b, l: (b * nl + l, 0))
    out_shapes = (
        jax.ShapeDtypeStruct((B, ATTN_HEADS, 128, L), _MXU),
        jax.ShapeDtypeStruct((n, 512), _MXU),
        jax.ShapeDtypeStruct((B, nl, 512, t), _MXU),
        jax.ShapeDtypeStruct((B, IDX_HEADS, 128, L), _MXU),
        jax.ShapeDtypeStruct((n, 128), _MXU),
        jax.ShapeDtypeStruct((B, IDX_HEADS, L), F32),
        jax.ShapeDtypeStruct((n, 512), _MXU),
    )
    out_specs = (
        pl.BlockSpec((None, ATTN_HEADS, 128, t), lambda b, l: (b, 0, 0, l)),
        tok(512),
        pl.BlockSpec((None, None, 512, t), lambda b, l: (b, l, 0, 0)),
        pl.BlockSpec((None, IDX_HEADS, 128, t), lambda b, l: (b, 0, 0, l)),
        tok(128),
        pl.BlockSpec((None, IDX_HEADS, t), lambda b, l: (b, 0, l)),
        tok(512),
    )
    return pl.pallas_call(
        functools.partial(_mix_in_kernel, att_scale=att_scale, idx_scale=idx_scale),
        grid=(B, nl),
        in_specs=[tok(d),
                  pl.BlockSpec((None, 6, d), lambda b, l: (b, 0, 0)),
                  full((1, d)), full(w_cat.shape), full((1, 512)), full((1, 512)),
                  full((3, 512)), full((1, 512)), full((1, 512)), full((512, 512))],
        out_specs=out_specs,
        out_shape=out_shapes,
        scratch_shapes=[pltpu.VMEM((8, 512), F32)],
        compiler_params=_cparams(("arbitrary", "arbitrary")),
        name="mix_in",
    )(x2, mod3, norm1_w, w_cat, qnw, knw, conv_w, conv_b, cnw, ones_bd)


def _dsa_kernel(qT_ref, iqT_ref, iwT_ref, k_ref, vT_ref, ik_ref, anw_ref, qnw_ref, knw_ref, o_ref,
                I_ref, m_ref, l_ref, acc_ref, bias_ref, s_ref, *, topk, seq_len, att_scale):
    t = DSA_TQ
    kc = TILE
    ndiag = t // kc
    qb = pl.program_id(1)
    nk = (qb + 1) * ndiag
    kf = float(topk)
    inf = float("inf")
    qpos = qb * t + lax.broadcasted_iota(jnp.int32, (1, t), 1)
    rowi = lax.broadcasted_iota(jnp.int32, (kc, 1), 0)

    def scores(c):
        ikc = ik_ref[c]
        acc = jnp.zeros((kc, t), F32)
        for h in range(IDX_HEADS):
            r = _dot(ikc, iqT_ref[h])
            acc = acc + iwT_ref[h:h + 1, :] * jnp.maximum(r, 0.0)
        return acc

    def p1(c, carry):
        mn, mx = carry
        acc = scores(c)
        I_ref[c] = acc
        return (jnp.minimum(mn, jnp.min(acc, axis=0, keepdims=True)),
                jnp.maximum(mx, jnp.max(acc, axis=0, keepdims=True)))

    mn, mx = lax.fori_loop(0, nk - ndiag, p1, (jnp.full((1, t), inf, F32), jnp.full((1, t), -inf, F32)))
    for dchunk in range(ndiag):
        c = nk - ndiag + dchunk
        acc = scores(c)
        causal = (c * kc + rowi) <= qpos
        I_ref[c] = jnp.where(causal, acc, -inf)
        mn = jnp.minimum(mn, jnp.min(jnp.where(causal, acc, inf), axis=0, keepdims=True))
        mx = jnp.maximum(mx, jnp.max(jnp.where(causal, acc, -inf), axis=0, keepdims=True))

    nk2 = (nk + 1) // 2
    nkg = (nk + SCAN_GROUP - 1) // SCAN_GROUP
    for d in range(SCAN_GROUP - 1):
        @pl.when(nk + d < nkg * SCAN_GROUP)
        def _(d=d):
            I_ref[nk + d] = jnp.full((kc, t), -inf, F32)

    slab = 32
    per_chunk = kc // slab

    def scan_pairs(fn, init):
        def body(j, s):
            for m in range(SCAN_GROUP * per_chunk):
                u, r = divmod(m, per_chunk)
                blk = I_ref[SCAN_GROUP * j + u, r * slab:(r + 1) * slab, :]
                s = fn(s, blk, (SCAN_GROUP * j + u) * kc + r * slab)
            return s
        return lax.fori_loop(0, nkg, body, init)

    row_s = lax.broadcasted_iota(jnp.int32, (slab, 1), 0)

    def count(pred):
        s = scan_pairs(lambda s, blk, base: jnp.where(pred(blk, base + row_s), s + 1.0, s),
                       jnp.zeros((slab, t), F32))
        return jnp.sum(s, axis=0, keepdims=True)

    def min_where(pred):
        s = scan_pairs(lambda s, blk, base: jnp.minimum(s, jnp.where(pred(blk), blk, inf)),
                       jnp.full((slab, t), inf, F32))
        return jnp.min(s, axis=0, keepdims=True)
    done0 = jnp.where(qpos + 1 > topk, 0.0, 1.0)
    tau0 = jnp.full((1, t), -inf, F32)
    hi0 = mx + (jnp.abs(mx) + 1.0)

    def cond_a(st):
        it, _, _, _, done = st
        return jnp.logical_and(it < 40, jnp.min(done) < 0.5)

    def body_a(st):
        it, lo, hi, tau, done = st
        mid = lo + (hi - lo) * 0.5
        c = count(lambda blk, _: blk >= mid)
        hit = jnp.logical_and(c == kf, done < 0.5)
        tau = jnp.where(hit, mid, tau)
        done = jnp.where(hit, 1.0, done)
        ge = c >= kf
        return it + 1, jnp.where(ge, mid, lo), jnp.where(ge, hi, mid), tau, done

    n_blind = jnp.where(qb * t + t > topk, BISECT_BLIND_STEPS, 0)
    st = lax.fori_loop(0, n_blind, lambda _, s: body_a(s), (jnp.int32(0), mn, hi0, tau0, done0))
    _, lo, hi, tau, done = lax.while_loop(cond_a, body_a, st)

    def cond_b(st):
        it, _, _, _, done, _, _ = st
        return jnp.logical_and(it < 4096, jnp.min(done) < 0.5)

    def body_b(st):
        it, lo, hi, tau, done, tie, need = st
        vlo = min_where(lambda blk: blk >= lo)
        ngt = count(lambda blk, _: blk > vlo)
        is_tie = jnp.logical_and(ngt < kf, done < 0.5)
        tau = jnp.where(is_tie, vlo, tau)
        need = jnp.where(is_tie, kf - ngt, need)
        tie = jnp.where(is_tie, 1.0, tie)
        done = jnp.where(is_tie, 1.0, done)
        lo2 = min_where(lambda blk: blk > vlo)
        mid = lo2 + (hi - lo2) * 0.5
        c = count(lambda blk, _: blk >= mid)
        hit = jnp.logical_and(c == kf, done < 0.5)
        tau = jnp.where(hit, mid, tau)
        done = jnp.where(hit, 1.0, done)
        ge = c >= kf
        return it + 1, jnp.where(ge, mid, lo2), jnp.where(ge, hi, mid), tau, done, tie, need

    zeros = jnp.zeros((1, t), F32)
    _, _, _, tau, done, tie, need = lax.while_loop(
        cond_b, body_b, (jnp.int32(0), lo, hi, tau, done, zeros, zeros))

    nbits = int(seq_len).bit_length() + 1
    ntrip = jnp.where(jnp.max(tie) > 0.5, nbits, 0)

    def body_j(_, st):
        jlo, jhi = st
        jm = (jlo + jhi) >> 1
        c = count(lambda blk, kpos: jnp.logical_and(blk == tau, kpos <= jm))
        ok = c >= need
        return jnp.where(ok, jlo, jm), jnp.where(ok, jm, jhi)

    _, jhi = lax.fori_loop(0, ntrip, body_j,
                           (jnp.full((1, t), -1, jnp.int32), jnp.full((1, t), seq_len - 1, jnp.int32)))
    jstar = jnp.where(tie > 0.5, jhi, -1)

    ones_rows = jnp.ones((16, kc), _MXU)

    @pl.when(ntrip > 0)
    def _():
        def drop_tail(c, carry):
            blk = I_ref[c]
            drop = jnp.logical_and(jnp.logical_and(blk == tau, (c * kc + rowi) > jstar), tie > 0.5)
            I_ref[c] = jnp.where(drop, -inf, blk)
            return carry
        lax.fori_loop(0, nk, drop_tail, 0)

    tau_eff = jnp.where(done0 > 0.5, -3.0e38, tau)

    def selected(c):
        return I_ref[c] >= tau_eff

    mb = (HEAD_DIM * att_scale) * (jnp.max(jnp.abs(qnw_ref[...]), axis=1, keepdims=True)
                                   * jnp.max(jnp.abs(knw_ref[...]), axis=1, keepdims=True))
    l_ref[...] = jnp.zeros(l_ref.shape, F32)
    acc_ref[...] = jnp.zeros(acc_ref.shape, F32)

    lane_w = 256

    def p3_bounded(j, carry):
        for u in range(2):
            c = 2 * j + u
            bias_ref[u] = jnp.where(selected(c), -mb, NEG)
            for half in range(t // lane_w):
                lanes = slice(half * lane_w, (half + 1) * lane_w)
                ls = []
                for h in range(ATTN_HEADS):
                    p = h // 2
                    rows = slice(h * HEAD_DIM, (h + 1) * HEAD_DIM)
                    s = _dot(k_ref[c, :, p * 128:(p + 1) * 128], qT_ref[h, :, lanes]) + bias_ref[u, :, lanes]
                    pb = jnp.exp2(s).astype(_MXU)
                    ls.append(_dot(ones_rows, pb)[0:1])
                    acc_ref[rows, lanes] += _dot(vT_ref[c, rows, :], pb)
                l_ref[:, lanes] += jnp.concatenate(ls, axis=0)
        return carry

    lax.fori_loop(0, nk2, p3_bounded, 0)

    underflow = jnp.logical_not(jnp.min(l_ref[...]) > 2.0 ** -60)

    def p3(c, carry):
        bias_ref[0] = jnp.where(selected(c), 0.0, NEG)
        mcs = []
        for h in range(ATTN_HEADS):
            p = h // 2
            s = _dot(k_ref[c, :, p * 128:(p + 1) * 128], qT_ref[h]) + bias_ref[0]
            s_ref[h] = s
            mcs.append(jnp.max(s, axis=0, keepdims=True))
        m_old = m_ref[...]
        m_new = jnp.maximum(m_old, jnp.concatenate(mcs, axis=0))
        alpha = jnp.exp2(m_old - m_new)
        m_ref[...] = m_new
        ls = []
        for h in range(ATTN_HEADS):
            rows = slice(h * HEAD_DIM, (h + 1) * HEAD_DIM)
            pb = jnp.exp2(s_ref[h] - m_new[h:h + 1]).astype(_MXU)
            ls.append(_dot(ones_rows, pb)[0:1])
            acc_ref[rows, :] = alpha[h:h + 1] * acc_ref[rows, :] + _dot(vT_ref[c, rows, :], pb)
        l_ref[...] = alpha * l_ref[...] + jnp.concatenate(ls, axis=0)
        return carry

    @pl.when(underflow)
    def _():
        m_ref[...] = jnp.full(m_ref.shape, NEG, F32)
        l_ref[...] = jnp.zeros(l_ref.shape, F32)
        acc_ref[...] = jnp.zeros(acc_ref.shape, F32)
        lax.fori_loop(0, nk, p3, 0)

    for h in range(ATTN_HEADS):
        acc_ref[h * HEAD_DIM:(h + 1) * HEAD_DIM, :] = acc_ref[h * HEAD_DIM:(h + 1) * HEAD_DIM, :] / l_ref[h:h + 1, :]
    yt = acc_ref[...]
    ms = jnp.mean(yt * yt, axis=0, keepdims=True)
    o_ref[...] = (yt * lax.rsqrt(ms + EPS) * anw_ref[...]).T.astype(o_ref.dtype)


def _dsa(qT, iqT, iwT, k4, vT4, ik4, anw_col, qnw, knw, *, B, L, topk, att_scale):
    t = DSA_TQ
    kc = TILE
    nl = L // kc
    nq = L // t
    n = B * L
    return pl.pallas_call(
        functools.partial(_dsa_kernel, topk=topk, seq_len=L, att_scale=att_scale),
        grid=(B, nq),
        in_specs=[pl.BlockSpec((None, ATTN_HEADS, 128, t), lambda b, q: (b, 0, 0, q)),
                  pl.BlockSpec((None, IDX_HEADS, 128, t), lambda b, q: (b, 0, 0, q)),
                  pl.BlockSpec((None, IDX_HEADS, t), lambda b, q: (b, 0, q)),
                  pl.BlockSpec((None, nl, kc, 512), lambda b, q: (b, 0, 0, 0), pipeline_mode=pl.Buffered(1)),
                  pl.BlockSpec((None, nl, 512, kc), lambda b, q: (b, 0, 0, 0), pipeline_mode=pl.Buffered(1)),
                  pl.BlockSpec((None, nl, kc, 128), lambda b, q: (b, 0, 0, 0), pipeline_mode=pl.Buffered(1)),
                  pl.BlockSpec((512, 1), lambda b, q: (0, 0)),
                  pl.BlockSpec((1, 512), lambda b, q: (0, 0)),
                  pl.BlockSpec((1, 512), lambda b, q: (0, 0))],
        out_specs=pl.BlockSpec((t, 512), lambda b, q: (b * nq + q, 0)),
        out_shape=jax.ShapeDtypeStruct((n, 512), _MXU),
        scratch_shapes=[pltpu.VMEM((nl, kc, t), F32),
                        pltpu.VMEM((ATTN_HEADS, t), F32),
                        pltpu.VMEM((ATTN_HEADS, t), F32),
                        pltpu.VMEM((512, t), F32),
                        pltpu.VMEM((2, kc, t), F32),
                        pltpu.VMEM((ATTN_HEADS, kc, t), F32)],
        compiler_params=_cparams(("arbitrary", "arbitrary")),
        name="dsa",
    )(qT, iqT, iwT, k4, vT4, ik4, anw_col, qnw, knw)


def _mix_out_kernel(x_ref, ya_ref, yc_ref, mod_ref, woa_ref, woc_ref, n2_ref, wqT_ref, sk_ref,
                    x1_ref, h2T_ref, r2_ref, f2_ref, c1_ref, e1_ref, sT_ref):
    proj = _dot(ya_ref[...], woa_ref[...]) + _dot(yc_ref[...], woc_ref[...])
    x1 = x_ref[...] + mod_ref[2:3, :] * proj
    x1_ref[...] = x1
    ms = jnp.mean(x1 * x1, axis=-1, keepdims=True)
    h2 = x1 * lax.rsqrt(ms + EPS) * n2_ref[...] * (1.0 + mod_ref[4:5, :]) + mod_ref[3:4, :]
    h2t = h2.T.astype(_MXU)
    h2T_ref[...] = h2t
    qpt = _dot(wqT_ref[...], h2t).astype(_MXU)
    for hp in range(2 * PEER_HEADS):
        sT_ref[hp] = _dot(sk_ref[hp], qpt[hp * 128:(hp + 1) * 128])
    _route_tile(sT_ref, r2_ref, f2_ref, c1_ref, e1_ref)


def _mix_out(x2, ya, yc, mod3, woa, woc, norm2_w, wqT, sk, *, B, L):
    n, d = x2.shape
    t = TILE
    nl = L // t
    full = lambda shape: pl.BlockSpec(shape, lambda i: (0,) * len(shape))
    rspec = pl.BlockSpec((PEER_HEADS, PEER_KEYS, t), lambda i: (0, 0, i))
    rshp = lambda dt: jax.ShapeDtypeStruct((PEER_HEADS, PEER_KEYS, n), dt)
    return pl.pallas_call(
        _mix_out_kernel,
        grid=(n // t,),
        in_specs=[pl.BlockSpec((t, d), lambda i: (i, 0)),
                  pl.BlockSpec((t, 512), lambda i: (i, 0)),
                  pl.BlockSpec((t, 512), lambda i: (i, 0)),
                  pl.BlockSpec((None, 6, d), lambda i: (i // nl, 0, 0)),
                  full(woa.shape), full(woc.shape), full((1, d)), full(wqT.shape), full(sk.shape)],
        out_specs=(pl.BlockSpec((t, d), lambda i: (i, 0)),
                   pl.BlockSpec((d, t), lambda i: (0, i)),
                   rspec, rspec, rspec, rspec),
        out_shape=(jax.ShapeDtypeStruct((n, d), F32),
                   jax.ShapeDtypeStruct((d, n), _MXU),
                   rshp(_MXU), rshp(_MXU), rshp(F32), rshp(F32)),
        scratch_shapes=[pltpu.VMEM((2 * PEER_HEADS, PEER_KEYS, t), F32)],
        compiler_params=_cparams(("arbitrary",)),
        name="mix_out",
    )(x2, ya, yc, mod3, woa, woc, norm2_w, wqT, sk)


def _extract_topk(vals, n_take, tie_break):
    r, t = vals.shape
    rowi = lax.broadcasted_iota(jnp.int32, (r, t), 0) if tie_break else None
    work = vals
    rank = jnp.full((r, t), 99.0, F32)
    taken = []
    for kk in range(n_take):
        m = jnp.max(work, axis=0, keepdims=True)
        sel = work == m
        if tie_break:
            sel = rowi == jnp.min(jnp.where(sel, rowi, r), axis=0, keepdims=True)
        rank = jnp.where(sel, float(kk), rank)
        work = jnp.where(sel, -float("inf"), work)
        taken.append(m)
    return taken, rank


_COMBO_B_LIMIT = {1: 8, 2: 5, 3: 4, 4: 3, 5: 2, 6: 2, 7: 2}


def _route_head(s1, s2, tie_break):
    kk = PEER_TOPK
    ninf = -float("inf")
    v1, rank1 = _extract_topk(s1, kk, tie_break)
    v2, rank2 = _extract_topk(s2, kk, tie_break)
    v1all = jnp.concatenate(v1, axis=0)
    v2all = jnp.concatenate(v2, axis=0)
    sub = lax.broadcasted_iota(jnp.int32, (8, s1.shape[1]), 0)
    pieces = [v1[0] + v2all]
    for a in range(1, 8):
        piece = v1[a] + v2all[0:8]
        if _COMBO_B_LIMIT[a] < 8:
            piece = jnp.where(sub < _COMBO_B_LIMIT[a], piece, ninf)
        pieces.append(piece)
    pieces.append(v1all[8:16] + v2[0])
    combo = jnp.concatenate(pieces, axis=0)
    _, crank = _extract_topk(combo, kk, True)
    selc = jnp.where(crank < float(kk), 1.0, 0.0)
    z = jnp.sum(selc * jnp.exp(combo - combo[0:1]), axis=0, keepdims=True)
    cnts = [jnp.sum(selc[0:16], axis=0, keepdims=True)]
    cnts += [jnp.sum(selc[8 + 8 * a:16 + 8 * a], axis=0, keepdims=True) for a in range(1, 8)]
    cnts += [selc[72 + a:73 + a] for a in range(8)]
    cnt1 = jnp.zeros_like(s1)
    for a in range(kk):
        cnt1 = cnt1 + jnp.where(rank1 == float(a), cnts[a], 0.0)
    f2 = jnp.where(rank2 < float(kk), jnp.exp(s2 - v2[0]), 0.0)
    e1 = jnp.where(rank1 < float(kk), jnp.exp(s1 - v1[0]), 0.0) / z
    n_ranked = (jnp.sum(jnp.where(rank1 < float(kk), 1.0, 0.0), axis=0, keepdims=True)
                + jnp.sum(jnp.where(rank2 < float(kk), 1.0, 0.0), axis=0, keepdims=True))
    return rank2, f2, cnt1, e1, n_ranked


def _route_tile(sT_ref, r2_ref, f2_ref, c1_ref, e1_ref):
    def run(tie_break):
        worst = jnp.zeros((1, sT_ref.shape[2]), F32)
        for h in range(PEER_HEADS):
            rank2, f2, cnt1, e1, n_ranked = _route_head(sT_ref[2 * h], sT_ref[2 * h + 1], tie_break)
            r2_ref[h] = rank2.astype(r2_ref.dtype)
            f2_ref[h] = f2.astype(f2_ref.dtype)
            c1_ref[h] = cnt1
            e1_ref[h] = e1
            worst = jnp.maximum(worst, n_ranked)
        return jnp.max(worst)

    most_ranked = run(False)

    @pl.when(most_ranked > 2.0 * PEER_TOPK)
    def _():
        run(True)


PEER_TM = 512
PEER_TE = 2048


def _gelu(a):
    c0 = 0.7978845608028654
    inner = a * (c0 + (c0 * 0.044715) * (a * a))
    return (0.5 * a) * (1.0 + jnp.tanh(inner))


def _peer_kernel(h2T_ref, r2_ref, f2_ref, c1_ref, e1_ref, u_ref, vT_ref, x1_ref, mod_ref, o_ref,
                 acc_ref, g_ref):
    e = pl.program_id(1)
    sub = PEER_TE // PEER_KEYS

    @pl.when(e == 0)
    def _():
        acc_ref[...] = jnp.zeros_like(acc_ref)

    for ii in range(sub):
        rows = slice(ii * PEER_KEYS, (ii + 1) * PEER_KEYS)
        a = _dot(u_ref[rows, :], h2T_ref[...]).astype(_MXU)
        i = e * sub + ii
        w = jnp.zeros((PEER_KEYS, PEER_TM), _MXU)
        for h in range(PEER_HEADS):
            c1 = jnp.tile(jnp.broadcast_to(c1_ref[h, pl.ds(i, 1), :], (16, PEER_TM)).astype(_MXU), (8, 1))
            e1 = jnp.tile(jnp.broadcast_to(e1_ref[h, pl.ds(i, 1), :], (16, PEER_TM)).astype(_MXU), (8, 1))
            w = w + jnp.where(r2_ref[h] < c1, f2_ref[h], jnp.zeros((), _MXU)) * e1
        g_ref[rows, :] = _gelu(a) * w
    acc_ref[...] += _dot(vT_ref[...], g_ref[...])

    @pl.when(e == pl.num_programs(1) - 1)
    def _():
        o_ref[...] = x1_ref[...] + mod_ref[5:6, :] * acc_ref[...].T


def _peer(h2T, r2, f2, c1, e1, u_b, vT_b, x1, mod3, *, B, L):
    d, n = h2T.shape
    ne = u_b.shape[0]
    tm, te = PEER_TM, PEER_TE
    ntl = L // tm
    rspec = pl.BlockSpec((PEER_HEADS, PEER_KEYS, tm), lambda i, e: (0, 0, i))
    return pl.pallas_call(
        _peer_kernel,
        grid=(n // tm, ne // te),
        in_specs=[pl.BlockSpec((d, tm), lambda i, e: (0, i)),
                  rspec, rspec, rspec, rspec,
                  pl.BlockSpec((te, d), lambda i, e: (e, 0)),
                  pl.BlockSpec((d, te), lambda i, e: (0, e)),
                  pl.BlockSpec((tm, d), lambda i, e: (i, 0)),
                  pl.BlockSpec((None, 6, d), lambda i, e: (i // ntl, 0, 0))],
        out_specs=pl.BlockSpec((tm, d), lambda i, e: (i, 0)),
        out_shape=jax.ShapeDtypeStruct((n, d), F32),
        scratch_shapes=[pltpu.VMEM((d, tm), F32), pltpu.VMEM((te, tm), _MXU)],
        compiler_params=_cparams(("arbitrary", "arbitrary")),
        name="peer",
    )(h2T, r2, f2, c1, e1, u_b, vT_b, x1, mod3)


def kernel(x, c, norm1_w, norm2_w, w_ada, b_ada, w_in, q_norm_w, k_norm_w, conv_w, conv_b,
           attn_out_norm_w, conv_out_norm_w, w_out, peer_wq, peer_subkeys, peer_u, peer_v):
    B, L, D = x.shape
    n = B * L
    assert D == 1024 and L % PEER_TM == 0 and L % (SCAN_GROUP * TILE) == 0 and L % DSA_TQ == 0
    topk = min(MAX_TOPK, L // 4)
    att_scale = HEAD_DIM ** -0.5 * LOG2E
    idx_scale = (IDX_DIM ** -0.5) * (IDX_HEADS ** -0.5)

    c_pad = jnp.pad(c, ((0, 8 - B % 8 if B % 8 else 0), (0, 0)))
    w_ik = w_in[:, 3584:3648]
    w_iw = jnp.pad(w_in[:, 3648:3656], ((0, 0), (0, 120)))
    w_cat = jnp.concatenate([w_in[:, :3584], w_ik, w_ik, w_iw], axis=1).astype(_MXU)
    tile8 = lambda w: jnp.tile(w, ATTN_HEADS).reshape(1, 512)
    head_id = jnp.arange(512) // HEAD_DIM
    ones_bd = (head_id[:, None] == head_id[None, :]).astype(_MXU)
    woa = w_out[:512].astype(_MXU)
    woc = w_out[512:].astype(_MXU)
    wqT = peer_wq.T.astype(_MXU)
    sk = peer_subkeys.reshape(2 * PEER_HEADS, PEER_KEYS, PEER_KEYS).astype(_MXU)
    u_b = peer_u.astype(_MXU)
    vT_b = peer_v.T.astype(_MXU)

    mod = _ada(c_pad, w_ada, b_ada.reshape(1, -1))[:B]
    mod3 = mod.reshape(B, 6, D)
    x2 = x.reshape(n, D)

    qT, k, vT4, iqT, ik, iwT, yc = _mix_in(
        x2, mod3, norm1_w.reshape(1, D), w_cat, tile8(q_norm_w), tile8(k_norm_w),
        conv_w, conv_b.reshape(1, 512), conv_out_norm_w.reshape(1, 512), ones_bd,
        B=B, L=L, att_scale=att_scale, idx_scale=idx_scale)
    nl = L // TILE
    ya = _dsa(qT, iqT, iwT, k.reshape(B, nl, TILE, 512), vT4, ik.reshape(B, nl, TILE, 128),
              attn_out_norm_w.reshape(512, 1), tile8(q_norm_w), tile8(k_norm_w),
              B=B, L=L, topk=topk, att_scale=att_scale)
    x1, h2T, r2, f2, c1, e1 = _mix_out(x2, ya, yc, mod3, woa, woc, norm2_w.reshape(1, D), wqT, sk, B=B, L=L)
    out = _peer(h2T, r2, f2, c1, e1, u_b, vT_b, x1, mod3, B=B, L=L)
    return out.reshape(B, L, D)
```
